```python
import jax, jax.numpy as jnp
from jax import lax
import numpy as np

D_MODEL = 2048
BATCH = 1
SEQ = 16384
DEPTH = 4
DEC_BATCH = 8
DEC_SEQ = 64
PAST_LEN = 2048

CHUNK = 64
N_MIXERS = 3
EPS = 1e-6
POOL_WINDOWS = (2, 4, 8, 16)
N_POOL_GROUPS = 4
POOL_GROUP = D_MODEL // N_POOL_GROUPS
POOL_STATE = 16 - 1
CONV_WIDTH = 31
CONV_STATE = CONV_WIDTH - 1
GM_BLOCK = 128
GM_GROUPS = 4
D_GM = 2 * D_MODEL
GM_GROUP = D_GM // GM_GROUPS
MEM_LEN = 256
N_XHEADS = 4
XHEAD_DIM = D_MODEL // N_XHEADS
D_FF = 11 * D_MODEL // 4
N_EXPERTS = 8
TOP_K = 2
D_FF_EXPERT = D_FF // 2
N_POOL_LAYERS = (DEPTH + 2) // 3
N_CONV_LAYERS = (DEPTH + 1) // 3
N_GM_LAYERS = DEPTH // 3
N_DENSE = (DEPTH + 1) // 2
N_MOE = DEPTH // 2

kernel_name = 'hybrid_streaming_pool_conv_gmlp_memxattn_moe_step'


def _rmsnorm(x, g):
    xf = x.astype(jnp.float32)
    y = xf * lax.rsqrt(jnp.mean(xf * xf, axis=-1, keepdims=True) + EPS)
    return (y * g.astype(jnp.float32)).astype(x.dtype)


def _pool_mix(h, prev, start_pos, w_grp, scale):
    b, t, _ = h.shape
    hp = jnp.concatenate([prev.astype(h.dtype), h], axis=1)
    cs = jnp.cumsum(hp.astype(jnp.float32), axis=1)
    cs = jnp.concatenate([jnp.zeros((b, 1, D_MODEL), jnp.float32), cs], axis=1)
    pos = start_pos + jnp.arange(t)
    end = cs[:, POOL_STATE + 1:]
    means = []
    for g, w in enumerate(POOL_WINDOWS):
        lo, hi = g * POOL_GROUP, (g + 1) * POOL_GROUP
        s = end[:, :, lo:hi] - cs[:, POOL_STATE + 1 - w:POOL_STATE + 1 - w + t, lo:hi]
        cnt = jnp.minimum(pos + 1, w).astype(jnp.float32)[None, :, None]
        means.append(s / cnt)
    mean = jnp.concatenate(means, axis=-1).astype(h.dtype)
    d = (mean - h).reshape(b, t, N_POOL_GROUPS, POOL_GROUP)
    out = jnp.einsum('btgc,gce->btge', d, w_grp).reshape(b, t, D_MODEL)
    return out * scale, hp[:, -POOL_STATE:]


def _conv_module(h, prev, w_in, w_dw, b_dw, g_norm, w_out):
    a, gate = jnp.split(h @ w_in, 2, axis=-1)
    u = a * jax.nn.sigmoid(gate)
    up = jnp.concatenate([prev.astype(u.dtype), u], axis=1)
    c = lax.conv_general_dilated(up, w_dw[:, None, :], window_strides=(1,), padding='VALID',
                                 dimension_numbers=('NWC', 'WIO', 'NWC'),
                                 feature_group_count=D_MODEL) + b_dw
    c = jax.nn.silu(_rmsnorm(c, g_norm))
    return c @ w_out, up[:, -CONV_STATE:]


def _gmlp_mix(h, w_in, g_v, w_s, b_s, w_out):
    b, t, _ = h.shape
    u, v = jnp.split(jax.nn.gelu(h @ w_in), 2, axis=-1)
    v = _rmsnorm(v, g_v)
    blk = min(t, GM_BLOCK)
    vb = v.reshape(b, t // blk, blk, GM_GROUPS, GM_GROUP)
    cidx = jnp.arange(blk) // CHUNK
    mask = cidx[None, :] <= cidx[:, None]
    ws = jnp.where(mask[None], w_s[:, :blk, :blk], 0)
    s = jnp.einsum('gij,bnjgc->bnigc', ws, vb) + b_s[:, :blk].T[None, None, :, :, None]
    y = u * s.reshape(b, t, D_GM)
    return y @ w_out, v


def _mem_kv(mem, g_mem, w_kv):
    b, m, _ = mem.shape
    k, v = jnp.split(_rmsnorm(mem, g_mem) @ w_kv, 2, axis=-1)
    return k.reshape(b, m, N_XHEADS, XHEAD_DIM), v.reshape(b, m, N_XHEADS, XHEAD_DIM)


def _cross_attn(h, k, v, w_q, w_o):
    b, t, _ = h.shape
    q = (h @ w_q).reshape(b, t, N_XHEADS, XHEAD_DIM)
    s = jnp.einsum('bthd,bmhd->bhtm', q, k).astype(jnp.float32) * (XHEAD_DIM ** -0.5)
    p = jax.nn.softmax(s, axis=-1).astype(h.dtype)
    o = jnp.einsum('bhtm,bmhd->bthd', p, v).reshape(b, t, D_MODEL)
    return o @ w_o


def _swiglu(h, w1, w3, w2):
    return (jax.nn.silu(h @ w1) * (h @ w3)) @ w2


def _moe(h, w_r, w1, w3, w2):
    logits = (h @ w_r).astype(jnp.float32)
    top_v, top_i = lax.top_k(logits, TOP_K)
    gates = jax.nn.softmax(top_v, axis=-1)
    gate_e = jnp.sum(jax.nn.one_hot(top_i, N_EXPERTS, dtype=jnp.float32) * gates[..., None],
                     axis=-2).astype(h.dtype)
    y = jnp.zeros_like(h)
    for e in range(N_EXPERTS):
        y = y + gate_e[..., e:e + 1] * _swiglu(h, w1[e], w3[e], w2[e])
    return y


def _trunk(x, start_pos, pool_prev, conv_prev, mem_k, mem_v, p):
    pool_new, conv_new, gm_new = [], [], []
    for i in range(DEPTH):
        slot = i // N_MIXERS
        h = _rmsnorm(x, p['norm_mix'][i])
        if i % N_MIXERS == 0:
            o, st = _pool_mix(h, pool_prev[slot], start_pos, p['pool_w'][slot], p['pool_scale'][slot])
            pool_new.append(st)
        elif i % N_MIXERS == 1:
            o, st = _conv_module(h, conv_prev[slot], p['conv_w_in'][slot], p['conv_w_dw'][slot],
                                 p['conv_b_dw'][slot], p['conv_norm'][slot], p['conv_w_out'][slot])
            conv_new.append(st)
        else:
            o, st = _gmlp_mix(h, p['gm_w_in'][slot], p['gm_norm'][slot], p['gm_w_s'][slot],
                              p['gm_b_s'][slot], p['gm_w_out'][slot])
            gm_new.append(st)
        x = x + o
        x = x + _cross_attn(_rmsnorm(x, p['norm_xattn'][i]), mem_k[i], mem_v[i],
                            p['xa_w_q'][i], p['xa_w_o'][i])
        h = _rmsnorm(x, p['norm_ffn'][i])
        j = i // 2
        if i % 2 == 0:
            x = x + _swiglu(h, p['ffn_w1'][j], p['ffn_w3'][j], p['ffn_w2'][j])
        else:
            x = x + _moe(h, p['moe_router'][j], p['moe_w1'][j], p['moe_w3'][j], p['moe_w2'][j])
    return _rmsnorm(x, p['norm_out']), pool_new, conv_new, gm_new


def setup_inputs(seed: int = 0) -> dict:
    key = jax.random.key(seed)
    ks = jax.random.split(key, 40)
    D = D_MODEL

    def nrm(i, shape, scale):
        return jax.random.normal(ks[i], shape, jnp.float32) * scale

    def gain(i, shape):
        return 1.0 + 0.1 * jax.random.normal(ks[i], shape, jnp.float32)

    return {
        'x_prompt': nrm(0, (BATCH, SEQ, D), 1.0),
        'x_sample': nrm(1, (DEC_BATCH, DEC_SEQ, D), 1.0),
        'cache_mem_k': nrm(2, (DEPTH, DEC_BATCH, MEM_LEN, N_XHEADS, XHEAD_DIM), 1.0),
        'cache_mem_v': nrm(3, (DEPTH, DEC_BATCH, MEM_LEN, N_XHEADS, XHEAD_DIM), 1.0),
        'state_pool': nrm(4, (N_POOL_LAYERS, DEC_BATCH, POOL_STATE, D), 1.0),
        'state_conv': nrm(5, (N_CONV_LAYERS, DEC_BATCH, CONV_STATE, D), 0.5),
        'mem_prompt': nrm(6, (BATCH, MEM_LEN, D), 1.0),
        'norm_mix': gain(7, (DEPTH, D)),
        'norm_xattn': gain(8, (DEPTH, D)),
        'norm_mem': gain(9, (DEPTH, D)),
        'norm_ffn': gain(10, (DEPTH, D)),
        'norm_out': gain(11, (D,)),
        'pool_w': nrm(12, (N_POOL_LAYERS, N_POOL_GROUPS, POOL_GROUP, POOL_GROUP), POOL_GROUP ** -0.5),
        'pool_scale': gain(13, (N_POOL_LAYERS, D)),
        'conv_w_in': nrm(14, (N_CONV_LAYERS, D, 2 * D), D ** -0.5),
        'conv_w_dw': nrm(15, (N_CONV_LAYERS, CONV_WIDTH, D), CONV_WIDTH ** -0.5),
        'conv_b_dw': nrm(16, (N_CONV_LAYERS, D), 0.02),
        'conv_norm': gain(17, (N_CONV_LAYERS, D)),
        'conv_w_out': nrm(18, (N_CONV_LAYERS, D, D), D ** -0.5),
        'gm_w_in': nrm(19, (N_GM_LAYERS, D, 2 * D_GM), D ** -0.5),
        'gm_norm': gain(20, (N_GM_LAYERS, D_GM)),
        'gm_w_s': nrm(21, (N_GM_LAYERS, GM_GROUPS, GM_BLOCK, GM_BLOCK), GM_BLOCK ** -0.5),
        'gm_b_s': gain(22, (N_GM_LAYERS, GM_GROUPS, GM_BLOCK)),
        'gm_w_out': nrm(23, (N_GM_LAYERS, D_GM, D), D_GM ** -0.5),
        'xa_w_q': nrm(24, (DEPTH, D, D), D ** -0.5),
        'xa_w_kv': nrm(25, (DEPTH, D, 2 * D), D ** -0.5),
        'xa_w_o': nrm(26, (DEPTH, D, D), D ** -0.5),
        'ffn_w1': nrm(27, (N_DENSE, D, D_FF), D ** -0.5),
        'ffn_w3': nrm(28, (N_DENSE, D, D_FF), D ** -0.5),
        'ffn_w2': nrm(29, (N_DENSE, D_FF, D), D_FF ** -0.5),
        'moe_router': nrm(30, (N_MOE, D, N_EXPERTS), D ** -0.5),
        'moe_w1': nrm(31, (N_MOE, N_EXPERTS, D, D_FF_EXPERT), D ** -0.5),
        'moe_w3': nrm(32, (N_MOE, N_EXPERTS, D, D_FF_EXPERT), D ** -0.5),
        'moe_w2': nrm(33, (N_MOE, N_EXPERTS, D_FF_EXPERT, D), D_FF_EXPERT ** -0.5),
    }


def reference(x_prompt, x_sample, cache_mem_k, cache_mem_v, state_pool, state_conv, mem_prompt,
              norm_mix, norm_xattn, norm_mem, norm_ffn, norm_out, pool_w, pool_scale,
              conv_w_in, conv_w_dw, conv_b_dw, conv_norm, conv_w_out,
              gm_w_in, gm_norm, gm_w_s, gm_b_s, gm_w_out, xa_w_q, xa_w_kv, xa_w_o,
              ffn_w1, ffn_w3, ffn_w2, moe_router, moe_w1, moe_w3, moe_w2):
    p = dict(norm_mix=norm_mix, norm_xattn=norm_xattn, norm_ffn=norm_ffn, norm_out=norm_out,
             pool_w=pool_w, pool_scale=pool_scale, conv_w_in=conv_w_in, conv_w_dw=conv_w_dw,
             conv_b_dw=conv_b_dw, conv_norm=conv_norm, conv_w_out=conv_w_out,
             gm_w_in=gm_w_in, gm_norm=gm_norm, gm_w_s=gm_w_s, gm_b_s=gm_b_s, gm_w_out=gm_w_out,
             xa_w_q=xa_w_q, xa_w_o=xa_w_o, ffn_w1=ffn_w1, ffn_w3=ffn_w3, ffn_w2=ffn_w2,
             moe_router=moe_router, moe_w1=moe_w1, moe_w3=moe_w3, moe_w2=moe_w2)

    kv_p = [_mem_kv(mem_prompt, norm_mem[i], xa_w_kv[i]) for i in range(DEPTH)]
    new_mem_k_prompt = jnp.stack([k for k, _ in kv_p])
    new_mem_v_prompt = jnp.stack([v for _, v in kv_p])
    b_p = x_prompt.shape[0]
    pool_zero = jnp.zeros((N_POOL_LAYERS, b_p, POOL_STATE, D_MODEL), x_prompt.dtype)
    conv_zero = jnp.zeros((N_CONV_LAYERS, b_p, CONV_STATE, D_MODEL), x_prompt.dtype)
    y_prompt, pool_p, conv_p, _gm_p = _trunk(x_prompt, 0, pool_zero, conv_zero,
                                             new_mem_k_prompt, new_mem_v_prompt, p)

    y_sample, pool_s, conv_s, gm_s = _trunk(x_sample, PAST_LEN, state_pool, state_conv,
                                            cache_mem_k, cache_mem_v, p)

    new_pool_prompt = jnp.stack(pool_p)
    new_pool_sample = jnp.stack(pool_s)
    new_conv_prompt = jnp.stack(conv_p)
    new_conv_sample = jnp.stack(conv_s)
    new_gm_v_sample = jnp.stack(gm_s)
    return (y_prompt, y_sample, new_mem_k_prompt, new_mem_v_prompt, new_pool_prompt, new_pool_sample,
            new_conv_prompt, new_conv_sample, new_gm_v_sample)
```

```python
import functools
import math

import jax
import jax.numpy as jnp
from jax import lax
from jax.experimental import pallas as pl
from jax.experimental.pallas import tpu as pltpu

F32 = jnp.float32
BF16 = jnp.bfloat16

EPS = 1e-6
PAST_LEN = 2048
CHUNK = 64
GM_BLOCK = 128
GM_GROUPS = 4
POOL_WINDOWS = (2, 4, 8, 16)
TOP_K = 2

TILE_M = 512
HALO = 32
PAD = 8
ROW_CHUNK = 16
LANES = 128
VMEM_LIMIT = 60 * 1024 * 1024


def _params(n_grid):
    return pltpu.CompilerParams(
        dimension_semantics=("arbitrary",) * n_grid, vmem_limit_bytes=VMEM_LIMIT)


def _pick(n, prefs):
    for p in prefs:
        if n % p == 0:
            return p
    return n


def _rms_rows(x_ref, g_ref, out_ref, n_rows, out_row0=0):
    def body(c, carry):
        r = pl.multiple_of(c * ROW_CHUNK, ROW_CHUNK)
        x = x_ref[pl.ds(r, ROW_CHUNK), :].astype(F32)
        ms = jnp.mean(x * x, axis=-1, keepdims=True)
        y = (x * lax.rsqrt(ms + EPS)) * g_ref[...]
        out_ref[pl.ds(out_row0 + r, ROW_CHUNK), :] = y.astype(out_ref.dtype)
        return carry
    lax.fori_loop(0, n_rows // ROW_CHUNK, body, 0)


def _cast_rows(x_ref, out_ref, n_rows):
    def body(c, carry):
        r = pl.multiple_of(c * ROW_CHUNK, ROW_CHUNK)
        out_ref[pl.ds(r, ROW_CHUNK), :] = x_ref[pl.ds(r, ROW_CHUNK), :].astype(out_ref.dtype)
        return carry
    lax.fori_loop(0, n_rows // ROW_CHUNK, body, 0)


def _sigmoid(x):
    return 1.0 / (1.0 + jnp.exp(-x))


def _silu(x):
    return x * _sigmoid(x)


def _gelu_tanh(x):
    c = math.sqrt(2.0 / math.pi)
    return 0.5 * x * (1.0 + jnp.tanh(c * (x + 0.044715 * (x * x * x))))


def _norm_matmul_kernel(*refs, n_w, epilogue, tm):
    x_ref, g_ref = refs[0], refs[1]
    w_refs = refs[2:2 + n_w]
    o_ref = refs[2 + n_w]
    h_ref = refs[3 + n_w]

    @pl.when(pl.program_id(1) == 0)
    def _():
        _rms_rows(x_ref, g_ref, h_ref, tm)

    h = h_ref[...]
    accs = [jnp.dot(h, w[...], preferred_element_type=F32) for w in w_refs]
    o_ref[...] = epilogue(*accs).astype(o_ref.dtype)


def _norm_matmul(x, g, ws, col_blocks, n_out, epilogue, out_dtype, tm, tn):
    m, k = x.shape
    n_w = len(ws)
    grid = (m // tm, n_out // tn)
    w_specs = [pl.BlockSpec((k, tn), functools.partial(lambda i, j, c0: (0, j + c0), c0=c0))
               for c0 in col_blocks]
    return pl.pallas_call(
        functools.partial(_norm_matmul_kernel, n_w=n_w, epilogue=epilogue, tm=tm),
        grid=grid,
        in_specs=[pl.BlockSpec((tm, k), lambda i, j: (i, 0)),
                  pl.BlockSpec((1, k), lambda i, j: (0, 0))] + w_specs,
        out_specs=pl.BlockSpec((tm, tn), lambda i, j: (i, j)),
        out_shape=jax.ShapeDtypeStruct((m, n_out), out_dtype),
        scratch_shapes=[pltpu.VMEM((tm, k), BF16)],
        compiler_params=_params(2),
        name="norm_matmul" + epilogue.__name__,
    )(x, g.reshape(1, k), *ws)


def _prologue_matmul_kernel(*refs, n_pro, n_extra, prologue):
    pro_refs = refs[:n_pro]
    w_ref, res_ref, o_ref = refs[n_pro], refs[n_pro + 1], refs[n_pro + 2]
    extra_refs = refs[n_pro + 3:n_pro + 3 + n_extra]
    a_ref = refs[n_pro + 3 + n_extra]
    scratch = refs[n_pro + 4 + n_extra:]

    @pl.when(pl.program_id(1) == 0)
    def _():
        prologue(pro_refs, extra_refs, a_ref, scratch)

    o_ref[...] = res_ref[...] + jnp.dot(a_ref[...], w_ref[...], preferred_element_type=F32)


def _prologue_matmul(x_res, w, pro_inputs, pro_specs, prologue, *, tile0, n_tiles, tm, tn,
                     extra_shapes=(), extra_specs=(), scratch_shapes=()):
    k, n = w.shape
    n_pro = len(pro_inputs)
    outs = pl.pallas_call(
        functools.partial(_prologue_matmul_kernel, n_pro=n_pro, n_extra=len(extra_shapes),
                          prologue=prologue),
        grid=(n_tiles, n // tn),
        in_specs=list(pro_specs) + [
            pl.BlockSpec((k, tn), lambda i, j: (0, j)),
            pl.BlockSpec((tm, tn), lambda i, j: (i + tile0, j))],
        out_specs=[pl.BlockSpec((tm, tn), lambda i, j: (i + tile0, j))] + list(extra_specs),
        out_shape=[jax.ShapeDtypeStruct(x_res.shape, x_res.dtype)] + list(extra_shapes),
        scratch_shapes=[pltpu.VMEM((tm, k), BF16)] + list(scratch_shapes),
        input_output_aliases={n_pro + 1: 0},
        compiler_params=_params(2),
        name=getattr(prologue, "func", prologue).__name__.strip("_") + ("_first" if tile0 == 0 else "_last"),
    )(*pro_inputs, w, x_res)
    return outs


def _pool_kernel(x_ref, g_ref, st_ref, w_ref, sc_ref, o_ref, tail_ref, hp_ref, sa_ref, sb_ref,
                 d_ref, *, tm, seg_len, carry, start_pos):
    d_model = x_ref.shape[1]
    n_groups = len(POOL_WINDOWS)
    gw = d_model // n_groups
    base = PAD + HALO
    n_seg = tm // seg_len
    i = pl.program_id(0)

    hp_ref[0:PAD, :] = jnp.zeros((PAD, d_model), F32)
    sa_ref[0:PAD, :] = jnp.zeros((PAD, gw), F32)
    sb_ref[0:PAD, :] = jnp.zeros((PAD, gw), F32)

    for s in range(n_seg):
        row0 = s * seg_len
        if carry:
            @pl.when(i == 0)
            def _():
                hp_ref[PAD:base, :] = jnp.zeros((HALO, d_model), F32)

            @pl.when(i > 0)
            def _():
                hp_ref[PAD:base, :] = hp_ref[PAD + seg_len:base + seg_len, :]
        else:
            hp_ref[PAD:base, :] = st_ref[s]

        def norm_body(c, cr):
            r = pl.multiple_of(c * ROW_CHUNK, ROW_CHUNK)
            x = x_ref[pl.ds(row0 + r, ROW_CHUNK), :]
            ms = jnp.mean(x * x, axis=-1, keepdims=True)
            hp_ref[pl.ds(base + r, ROW_CHUNK), :] = (x * lax.rsqrt(ms + EPS)) * g_ref[...]
            return cr
        lax.fori_loop(0, seg_len // ROW_CHUNK, norm_body, 0)

        tail_ref[s] = hp_ref[base + seg_len - 16:base + seg_len, :]

        if carry:
            pos0 = start_pos + i * tm + row0
        else:
            pos0 = start_pos
        pos = (pos0 + lax.broadcasted_iota(jnp.int32, (seg_len, 1), 0)).astype(F32)

        n_steps = (HALO + seg_len) // 32
        for g, win in enumerate(POOL_WINDOWS):
            c0 = g * gw
            bufs = (sa_ref, sb_ref)
            n_stage = int(math.log2(win))
            for t in range(n_stage):
                shift = 1 << t
                dst = bufs[t % 2]

                for c in range(n_steps):
                    r = PAD + c * 32
                    if t == 0:
                        a = hp_ref[r:r + 32, c0:c0 + gw]
                        b = hp_ref[r - shift:r - shift + 32, c0:c0 + gw]
                    else:
                        src = bufs[(t - 1) % 2]
                        a = src[r:r + 32, :]
                        b = src[r - shift:r - shift + 32, :]
                    dst[r:r + 32, :] = a + b
            sums = bufs[(n_stage - 1) % 2]
            cnt = jnp.minimum(pos + 1.0, float(win))
            mean = sums[base:base + seg_len, :] / cnt
            d_ref[row0:row0 + seg_len, c0:c0 + gw] = (
                mean - hp_ref[base:base + seg_len, c0:c0 + gw]).astype(BF16)

    for g in range(n_groups):
        c0 = g * gw
        mix = jnp.dot(d_ref[:, c0:c0 + gw], w_ref[g], preferred_element_type=F32)
        o_ref[:, c0:c0 + gw] = x_ref[:, c0:c0 + gw] + mix * sc_ref[:, c0:c0 + gw]


def _pool_mix(x_all, g, state, w_bf, scale, *, tile0, n_tiles, tm, seg_len, carry, start_pos):
    t_all, d = x_all.shape
    n_seg = tm // seg_len
    gw = d // len(POOL_WINDOWS)
    x_new, tail = pl.pallas_call(
        functools.partial(_pool_kernel, tm=tm, seg_len=seg_len, carry=carry, start_pos=start_pos),
        grid=(n_tiles,),
        in_specs=[pl.BlockSpec((tm, d), lambda i: (i + tile0, 0)),
                  pl.BlockSpec((1, d), lambda i: (0, 0)),
                  pl.BlockSpec(state.shape, lambda i: (0, 0, 0)),
                  pl.BlockSpec(w_bf.shape, lambda i: (0, 0, 0)),
                  pl.BlockSpec((1, d), lambda i: (0, 0))],
        out_specs=[pl.BlockSpec((tm, d), lambda i: (i + tile0, 0)),
                   pl.BlockSpec((n_seg, 16, d), lambda i: (0, 0, 0))],
        out_shape=[jax.ShapeDtypeStruct((t_all, d), F32),
                   jax.ShapeDtypeStruct((n_seg, 16, d), F32)],
        scratch_shapes=[pltpu.VMEM((PAD + HALO + seg_len, d), F32),
                        pltpu.VMEM((PAD + HALO + seg_len, gw), F32),
                        pltpu.VMEM((PAD + HALO + seg_len, gw), F32),
                        pltpu.VMEM((tm, d), BF16)],
        input_output_aliases={0: 0},
        compiler_params=_params(1),
        name="pool_mix_carry" if carry else "pool_mix_state",
    )(x_all, g.reshape(1, d), state, w_bf, scale.reshape(1, d))
    return x_new, tail


CONV_ROWS = 32
CONV_COLS = 256


def _conv_prologue(pro_refs, extra_refs, a_ref, scratch, *, tm, seg_len, carry, width):
    u_ref, st_ref, wdw_ref, bdw_ref, gn_ref = pro_refs
    up_ref, c_ref = scratch
    d_model = u_ref.shape[1]
    n_seg = tm // seg_len
    first = HALO - (width - 1)
    i = pl.program_id(0)

    for s in range(n_seg):
        row0 = s * seg_len
        if carry:
            @pl.when(i == 0)
            def _():
                up_ref[0:HALO, :] = jnp.zeros((HALO, d_model), F32)

            @pl.when(i > 0)
            def _():
                up_ref[0:HALO, :] = up_ref[seg_len:seg_len + HALO, :]
        else:
            up_ref[0:HALO, :] = st_ref[s]

        def copy_body(c, cr):
            r = pl.multiple_of(c * ROW_CHUNK, ROW_CHUNK)
            up_ref[pl.ds(HALO + r, ROW_CHUNK), :] = u_ref[pl.ds(row0 + r, ROW_CHUNK), :]
            return cr
        lax.fori_loop(0, seg_len // ROW_CHUNK, copy_body, 0)

        def row_body(rc, cr):
            r = pl.multiple_of(rc * CONV_ROWS, CONV_ROWS)
            ssq = jnp.zeros((CONV_ROWS, 1), F32)
            for cc in range(d_model // CONV_COLS):
                cols = slice(cc * CONV_COLS, (cc + 1) * CONV_COLS)
                win = up_ref[pl.ds(r, CONV_ROWS + HALO), cols]
                acc = jnp.zeros((CONV_ROWS, CONV_COLS), F32)
                for phase in range(8):
                    taps = [kk for kk in range(width) if (first + kk) % 8 == phase]
                    if not taps:
                        continue
                    sh = win if phase == 0 else win[phase:phase + CONV_ROWS + HALO - 8, :]
                    for kk in taps:
                        m = first + kk - phase
                        acc = acc + sh[m:m + CONV_ROWS, :] * wdw_ref[kk:kk + 1, cols]
                acc = acc + bdw_ref[:, cols]
                ssq = ssq + jnp.sum(acc * acc, axis=-1, keepdims=True)
                c_ref[pl.ds(r, CONV_ROWS), cols] = acc
            inv = lax.rsqrt(ssq / d_model + EPS)
            y = (c_ref[pl.ds(r, CONV_ROWS), :] * inv) * gn_ref[...]
            a_ref[pl.ds(row0 + r, CONV_ROWS), :] = _silu(y).astype(BF16)
            return cr
        lax.fori_loop(0, seg_len // CONV_ROWS, row_body, 0)


GM_COLS = 512


def _gmlp_prologue(pro_refs, extra_refs, a_ref, scratch, *, tm, emit_v):
    z_ref, gv_ref, ws_ref, bs_ref = pro_refs
    (vn_ref,) = scratch
    d_gm = gv_ref.shape[1]
    gcols = d_gm // GM_GROUPS

    def norm_body(c, cr):
        r = pl.multiple_of(c * ROW_CHUNK, ROW_CHUNK)
        ssq = jnp.zeros((ROW_CHUNK, 1), F32)
        for cc in range(d_gm // 1024):
            v = z_ref[pl.ds(r, ROW_CHUNK), d_gm + cc * 1024:d_gm + (cc + 1) * 1024].astype(F32)
            ssq = ssq + jnp.sum(v * v, axis=-1, keepdims=True)
        inv = lax.rsqrt(ssq / d_gm + EPS)
        for cc in range(d_gm // 1024):
            cols = slice(cc * 1024, (cc + 1) * 1024)
            v = z_ref[pl.ds(r, ROW_CHUNK), d_gm + cc * 1024:d_gm + (cc + 1) * 1024].astype(F32)
            vn = (v * inv) * gv_ref[:, cols]
            vn_ref[pl.ds(r, ROW_CHUNK), cols] = vn.astype(BF16)
            if emit_v:
                extra_refs[0][pl.ds(r, ROW_CHUNK), cols] = vn
        return cr
    lax.fori_loop(0, tm // ROW_CHUNK, norm_body, 0)

    for g in range(GM_GROUPS):
        for cc in range(gcols // GM_COLS):
            cols = slice(g * gcols + cc * GM_COLS, g * gcols + (cc + 1) * GM_COLS)
            s = jnp.dot(ws_ref[g], vn_ref[:, cols], preferred_element_type=F32)
            s = s + bs_ref[:, g:g + 1]
            a_ref[:, cols] = (z_ref[:, cols].astype(F32) * s).astype(BF16)


def _gmlp_spatial_weights(w_s, b_s, blk, tm):
    cidx = jnp.arange(blk) // CHUNK
    mask = cidx[None, :] <= cidx[:, None]
    ws = jnp.where(mask[None], w_s[:, :blk, :blk], 0.0)
    eye = jnp.eye(tm // blk, dtype=ws.dtype)
    big = jnp.einsum("ab,gij->gaibj", eye, ws).reshape(w_s.shape[0], tm, tm)
    bias = jnp.tile(b_s[:, :blk].T, (tm // blk, 1))
    return big.astype(BF16), bias.astype(F32)


def _attn_prologue(pro_refs, extra_refs, a_ref, scratch, *, segments, n_heads):
    q_ref, k_ref, v_ref = pro_refs
    d_model = q_ref.shape[1]
    hd = d_model // n_heads
    scale = hd ** -0.5
    for row0, n_rows, kv in segments:
        for h in range(n_heads):
            cols = slice(h * hd, (h + 1) * hd)
            q = q_ref[row0:row0 + n_rows, cols]
            s = lax.dot_general(q, k_ref[kv, :, cols], (((1,), (1,)), ((), ())),
                                preferred_element_type=F32) * scale
            m = jnp.max(s, axis=-1, keepdims=True)
            e = jnp.exp(s - m)
            p = e / jnp.sum(e, axis=-1, keepdims=True)
            o = jnp.dot(p.astype(BF16), v_ref[kv, :, cols], preferred_element_type=F32)
            a_ref[row0:row0 + n_rows, cols] = o.astype(BF16)


def _copy_prologue(pro_refs, extra_refs, a_ref, scratch, *, tm):
    _cast_rows(pro_refs[0], a_ref, tm)


def _router_kernel(x_ref, g_ref, wr_ref, h_ref, meta_ref, cnt_ref, run_ref, *, tm, n_experts):
    i = pl.program_id(0)

    @pl.when(i == 0)
    def _():
        run_ref[...] = jnp.zeros(run_ref.shape, F32)

    _rms_rows(x_ref, g_ref, h_ref, tm)
    logits = jnp.dot(h_ref[...], wr_ref[...], preferred_element_type=F32,
                     precision=lax.Precision.HIGHEST)
    lane = lax.broadcasted_iota(jnp.int32, (tm, LANES), 1)
    neg = jnp.float32(-jnp.inf)
    logits = jnp.where(lane < n_experts, logits, neg)
    m1 = jnp.max(logits, axis=-1, keepdims=True)
    i1 = jnp.min(jnp.where(logits == m1, lane, LANES), axis=-1, keepdims=True)
    rest = jnp.where(lane == i1, neg, logits)
    m2 = jnp.max(rest, axis=-1, keepdims=True)
    i2 = jnp.min(jnp.where(rest == m2, lane, LANES), axis=-1, keepdims=True)
    e2 = jnp.exp(m2 - m1)
    den = 1.0 + e2
    g1 = 1.0 / den
    g2 = e2 / den

    sel1 = lane == i1
    sel2 = lane == i2
    onehot = jnp.where(sel1 | sel2, 1.0, 0.0)
    r_i = lax.broadcasted_iota(jnp.int32, (tm, tm), 0)
    c_i = lax.broadcasted_iota(jnp.int32, (tm, tm), 1)
    tri = jnp.where(c_i < r_i, 1.0, 0.0).astype(BF16)
    before = jnp.dot(tri, onehot.astype(BF16), preferred_element_type=F32) + run_ref[0:1, :]
    rank1 = jnp.sum(jnp.where(sel1, before, 0.0), axis=-1, keepdims=True)
    rank2 = jnp.sum(jnp.where(sel2, before, 0.0), axis=-1, keepdims=True)
    run_ref[0:1, :] = run_ref[0:1, :] + jnp.sum(onehot, axis=0, keepdims=True)
    cnt_ref[...] = jnp.broadcast_to(run_ref[0:1, :], cnt_ref.shape)

    col = lax.broadcasted_iota(jnp.int32, (tm, 8), 1)
    meta = jnp.where(col == 0, i1.astype(F32),
           jnp.where(col == 1, i2.astype(F32),
           jnp.where(col == 2, rank1,
           jnp.where(col == 3, rank2,
           jnp.where(col == 4, g1,
           jnp.where(col == 5, g2, 0.0))))))
    meta_ref[...] = meta


def _router(x_all, g, w_r, tm):
    t_all, d = x_all.shape
    n_experts = w_r.shape[1]
    wr_pad = jnp.zeros((d, LANES), F32).at[:, :n_experts].set(w_r)
    return pl.pallas_call(
        functools.partial(_router_kernel, tm=tm, n_experts=n_experts),
        grid=(t_all // tm,),
        in_specs=[pl.BlockSpec((tm, d), lambda i: (i, 0)),
                  pl.BlockSpec((1, d), lambda i: (0, 0)),
                  pl.BlockSpec((d, LANES), lambda i: (0, 0))],
        out_specs=[pl.BlockSpec((tm, d), lambda i: (i, 0)),
                   pl.BlockSpec((tm, 8), lambda i: (i, 0)),
                   pl.BlockSpec((8, LANES), lambda i: (0, 0))],
        out_shape=[jax.ShapeDtypeStruct((t_all, d), F32),
                   jax.ShapeDtypeStruct((t_all, 8), F32),
                   jax.ShapeDtypeStruct((8, LANES), F32)],
        scratch_shapes=[pltpu.VMEM((8, LANES), F32)],
        compiler_params=_params(1),
        name="moe_router",
    )(x_all, g.reshape(1, d), wr_pad)


def _dispatch_kernel(p1_ref, p2_ref, h_ref, buf_in_ref, buf_ref, sem, *, tm):
    del buf_in_ref
    t0 = pl.program_id(0) * tm

    def copies(t):
        return (pltpu.make_async_copy(h_ref.at[pl.ds(t, 1)], buf_ref.at[pl.ds(p1_ref[t], 1)], sem),
                pltpu.make_async_copy(h_ref.at[pl.ds(t, 1)], buf_ref.at[pl.ds(p2_ref[t], 1)], sem))

    def start(r, cr):
        for c in copies(t0 + r):
            c.start()
        return cr
    lax.fori_loop(0, tm, start, 0)

    def wait(r, cr):
        for c in copies(t0 + r):
            c.wait()
        return cr
    lax.fori_loop(0, tm, wait, 0)


def _dispatch(h, p1, p2, n_slots, tm):
    t_all, d = h.shape
    buf0 = jnp.zeros((n_slots, d), h.dtype)
    return pl.pallas_call(
        functools.partial(_dispatch_kernel, tm=tm),
        grid_spec=pltpu.PrefetchScalarGridSpec(
            num_scalar_prefetch=2,
            grid=(t_all // tm,),
            in_specs=[pl.BlockSpec(memory_space=pl.ANY), pl.BlockSpec(memory_space=pl.ANY)],
            out_specs=pl.BlockSpec(memory_space=pl.ANY),
            scratch_shapes=[pltpu.SemaphoreType.DMA(())]),
        out_shape=jax.ShapeDtypeStruct((n_slots, d), h.dtype),
        input_output_aliases={3: 0},
        compiler_params=_params(1),
        name="moe_dispatch",
    )(p1, p2, h, buf0)


def _expert_up_kernel(te_ref, nu_ref, x_ref, w1_ref, w3_ref, o_ref, h_ref, *, tm):
    i = pl.program_id(0)

    @pl.when(i < nu_ref[0])
    def _():
        @pl.when(pl.program_id(1) == 0)
        def _():
            _cast_rows(x_ref, h_ref, tm)
        h = h_ref[...]
        a = jnp.dot(h, w1_ref[...], preferred_element_type=F32)
        b = jnp.dot(h, w3_ref[...], preferred_element_type=F32)
        o_ref[...] = (_silu(a) * b).astype(o_ref.dtype)

    @pl.when(i >= nu_ref[0])
    def _():
        o_ref[...] = jnp.zeros(o_ref.shape, o_ref.dtype)


def _expert_up(buf, w1, w3, tile_expert, n_used, tm, tn):
    n_slots, d = buf.shape
    f = w1.shape[2]
    return pl.pallas_call(
        functools.partial(_expert_up_kernel, tm=tm),
        grid_spec=pltpu.PrefetchScalarGridSpec(
            num_scalar_prefetch=2,
            grid=(n_slots // tm, f // tn),
            in_specs=[pl.BlockSpec((tm, d), lambda i, j, te, nu: (i, 0)),
                      pl.BlockSpec((None, d, tn), lambda i, j, te, nu: (te[i], 0, j)),
                      pl.BlockSpec((None, d, tn), lambda i, j, te, nu: (te[i], 0, j))],
            out_specs=pl.BlockSpec((tm, tn), lambda i, j, te, nu: (i, j)),
            scratch_shapes=[pltpu.VMEM((tm, d), BF16)]),
        out_shape=jax.ShapeDtypeStruct((n_slots, f), BF16),
        compiler_params=_params(2),
        name="moe_expert_up",
    )(tile_expert, n_used, buf, w1, w3)


def _expert_down_kernel(te_ref, nu_ref, x_ref, w_ref, o_ref):
    i = pl.program_id(0)

    @pl.when(i < nu_ref[0])
    def _():
        o_ref[...] = jnp.dot(x_ref[...], w_ref[...], preferred_element_type=F32)

    @pl.when(i >= nu_ref[0])
    def _():
        o_ref[...] = jnp.zeros(o_ref.shape, o_ref.dtype)


def _expert_down(t_sorted, w2, tile_expert, n_used, tm, tn):
    n_slots, f = t_sorted.shape
    d = w2.shape[2]
    return pl.pallas_call(
        _expert_down_kernel,
        grid_spec=pltpu.PrefetchScalarGridSpec(
            num_scalar_prefetch=2,
            grid=(n_slots // tm, d // tn),
            in_specs=[pl.BlockSpec((tm, f), lambda i, j, te, nu: (i, 0)),
                      pl.BlockSpec((None, f, tn), lambda i, j, te, nu: (te[i], 0, j))],
            out_specs=pl.BlockSpec((tm, tn), lambda i, j, te, nu: (i, j))),
        out_shape=jax.ShapeDtypeStruct((n_slots, d), F32),
        compiler_params=_params(2),
        name="moe_expert_down",
    )(tile_expert, n_used, t_sorted, w2)


def _combine_kernel(p1_ref, p2_ref, x_ref, gate_ref, y_ref, gout_ref, o_ref, a_ref, b_ref, sem, *,
                    tm, final_norm):
    t0 = pl.program_id(0) * tm

    def copies(r):
        t = t0 + r
        return (pltpu.make_async_copy(y_ref.at[pl.ds(p1_ref[t], 1)], a_ref.at[pl.ds(r, 1)], sem),
                pltpu.make_async_copy(y_ref.at[pl.ds(p2_ref[t], 1)], b_ref.at[pl.ds(r, 1)], sem))

    def start(r, cr):
        for c in copies(r):
            c.start()
        return cr
    lax.fori_loop(0, tm, start, 0)

    def wait(r, cr):
        for c in copies(r):
            c.wait()
        return cr
    lax.fori_loop(0, tm, wait, 0)

    def body(c, cr):
        r = pl.multiple_of(c * ROW_CHUNK, ROW_CHUNK)
        rows = pl.ds(r, ROW_CHUNK)
        g = gate_ref[rows, :]
        y = g[:, 4:5] * a_ref[rows, :] + g[:, 5:6] * b_ref[rows, :]
        x = x_ref[rows, :] + y
        if final_norm:
            ms = jnp.mean(x * x, axis=-1, keepdims=True)
            x = (x * lax.rsqrt(ms + EPS)) * gout_ref[...]
        o_ref[rows, :] = x
        return cr
    lax.fori_loop(0, tm // ROW_CHUNK, body, 0)


def _combine(x_all, meta, y_sorted, p1, p2, g_out, tm, final_norm):
    t_all, d = x_all.shape
    return pl.pallas_call(
        functools.partial(_combine_kernel, tm=tm, final_norm=final_norm),
        grid_spec=pltpu.PrefetchScalarGridSpec(
            num_scalar_prefetch=2,
            grid=(t_all // tm,),
            in_specs=[pl.BlockSpec((tm, d), lambda i, a, b: (i, 0)),
                      pl.BlockSpec((tm, 8), lambda i, a, b: (i, 0)),
                      pl.BlockSpec(memory_space=pl.ANY),
                      pl.BlockSpec((1, d), lambda i, a, b: (0, 0))],
            out_specs=pl.BlockSpec((tm, d), lambda i, a, b: (i, 0)),
            scratch_shapes=[pltpu.VMEM((tm, d), F32), pltpu.VMEM((tm, d), F32),
                            pltpu.SemaphoreType.DMA(())]),
        out_shape=jax.ShapeDtypeStruct((t_all, d), F32),
        input_output_aliases={2: 0},
        compiler_params=_params(1),
        name="moe_combine",
    )(p1, p2, x_all, meta, y_sorted, g_out.reshape(1, d))


def _moe_ffn(x_all, g, w_r, w1, w3, w2, g_out, tm, final_norm):
    t_all, d = x_all.shape
    n_experts = w_r.shape[1]
    h, meta, counts = _router(x_all, g, w_r, tm)

    cnt = counts[0, :n_experts].astype(jnp.int32)
    padded = ((cnt + tm - 1) // tm) * tm
    ends = jnp.cumsum(padded)
    offs = ends - padded
    e1 = meta[:, 0].astype(jnp.int32)
    e2 = meta[:, 1].astype(jnp.int32)
    p1 = offs[e1] + meta[:, 2].astype(jnp.int32)
    p2 = offs[e2] + meta[:, 3].astype(jnp.int32)
    n_tiles = (TOP_K * t_all) // tm + n_experts
    tile_start = jnp.arange(n_tiles, dtype=jnp.int32) * tm
    tile_expert = jnp.minimum(
        jnp.sum((tile_start[:, None] >= ends[None, :]).astype(jnp.int32), axis=1), n_experts - 1)
    n_used = (ends[-1] // tm).reshape(1).astype(jnp.int32)

    buf = _dispatch(h, p1, p2, n_tiles * tm, tm)
    f = w1.shape[2]
    t_sorted = _expert_up(buf, w1, w3, tile_expert, n_used, tm, _pick(f, (1408, 512, 256, 128)))
    y_sorted = _expert_down(t_sorted, w2, tile_expert, n_used, tm, _pick(d, (1024, 512, 256, 128)))
    return _combine(x_all, meta, y_sorted, p1, p2, g_out, tm, final_norm)


def _glu(a, gate):
    return a * _sigmoid(gate)


def _swiglu(a, b):
    return _silu(a) * b


def _identity(a):
    return a


def kernel(x_prompt, x_sample, cache_mem_k, cache_mem_v, state_pool, state_conv, mem_prompt,
           norm_mix, norm_xattn, norm_mem, norm_ffn, norm_out, pool_w, pool_scale,
           conv_w_in, conv_w_dw, conv_b_dw, conv_norm, conv_w_out,
           gm_w_in, gm_norm, gm_w_s, gm_b_s, gm_w_out, xa_w_q, xa_w_kv, xa_w_o,
           ffn_w1, ffn_w3, ffn_w2, moe_router, moe_w1, moe_w3, moe_w2):
    b_p, seq, d = x_prompt.shape
    b_s, dec_seq, _ = x_sample.shape
    depth = norm_mix.shape[0]
    mem_len = mem_prompt.shape[1]
    n_heads = cache_mem_k.shape[3]
    pool_state = state_pool.shape[2]
    conv_width = conv_w_dw.shape[1]
    conv_state = state_conv.shape[2]
    d_gm = gm_norm.shape[1]
    tm = TILE_M
    assert b_p == 1 and seq % tm == 0 and b_s * dec_seq == tm
    assert conv_state <= HALO and pool_state <= 16 and seq >= conv_state
    assert depth % 2 == 0, "the output norm is fused into the last layer's expert FFN"
    n_ptiles = seq // tm
    t_p = seq
    t_all = seq + tm

    x = jnp.concatenate([x_prompt.reshape(seq, d), x_sample.reshape(tm, d)], axis=0)
    bf = lambda a: a.astype(BF16)

    mem = mem_prompt.reshape(mem_len, d)
    kv_p = []
    for i in range(depth):
        kv = _norm_matmul(mem, norm_mem[i], [bf(xa_w_kv[i])], (0,), 2 * d, _identity, F32,
                          mem_len, _pick(2 * d, (1024, 512, 256, 128)))
        kv_p.append(kv)
    new_mem_k = jnp.stack([kv[:, :d] for kv in kv_p]).reshape(depth, 1, mem_len, n_heads, d // n_heads)
    new_mem_v = jnp.stack([kv[:, d:] for kv in kv_p]).reshape(depth, 1, mem_len, n_heads, d // n_heads)

    tn_d = _pick(d, (1024, 512, 256, 128))
    pool_tails_p, pool_tails_s, conv_p, conv_s, gm_s = [], [], [], [], []
    for i in range(depth):
        slot = i // 3
        mixer = i % 3
        if mixer == 0:
            w_bf = bf(pool_w[slot])
            zero_state = jnp.zeros((1, HALO, d), F32)
            x, tail_p = _pool_mix(x, norm_mix[i], zero_state, w_bf, pool_scale[slot], tile0=0,
                                  n_tiles=n_ptiles, tm=tm, seg_len=tm, carry=True, start_pos=0)
            st = jnp.zeros((b_s, HALO, d), F32).at[:, HALO - pool_state:].set(state_pool[slot])
            x, tail_s = _pool_mix(x, norm_mix[i], st, w_bf, pool_scale[slot], tile0=n_ptiles,
                                  n_tiles=1, tm=tm, seg_len=dec_seq, carry=False,
                                  start_pos=PAST_LEN)
            pool_tails_p.append(tail_p[:, 16 - pool_state:])
            pool_tails_s.append(tail_s[:, 16 - pool_state:])
        elif mixer == 1:
            w_in = bf(conv_w_in[slot])
            u = _norm_matmul(x, norm_mix[i], [w_in, w_in], (0, d // tn_d), d, _glu, F32,
                             tm, tn_d)
            wdw = jnp.zeros((HALO, d), F32).at[:conv_width].set(conv_w_dw[slot])
            w_out = bf(conv_w_out[slot])
            small = [wdw, conv_b_dw[slot].reshape(1, d), conv_norm[slot].reshape(1, d)]
            small_specs = [pl.BlockSpec((HALO, d), lambda i, j: (0, 0)),
                           pl.BlockSpec((1, d), lambda i, j: (0, 0)),
                           pl.BlockSpec((1, d), lambda i, j: (0, 0))]
            for tile0, n_tiles, seg_len, carry, st in (
                    (0, n_ptiles, tm, True, jnp.zeros((1, HALO, d), F32)),
                    (n_ptiles, 1, dec_seq, False,
                     jnp.zeros((b_s, HALO, d), F32).at[:, HALO - conv_state:].set(state_conv[slot]))):
                (x,) = _prologue_matmul(
                    x, w_out, [u, st] + small,
                    [pl.BlockSpec((tm, d), functools.partial(lambda i, j, t0: (i + t0, 0), t0=tile0)),
                     pl.BlockSpec(st.shape, lambda i, j: (0, 0, 0))] + small_specs,
                    functools.partial(_conv_prologue, tm=tm, seg_len=seg_len, carry=carry,
                                      width=conv_width),
                    tile0=tile0, n_tiles=n_tiles, tm=tm, tn=tn_d,
                    scratch_shapes=[pltpu.VMEM((HALO + seg_len, d), F32), pltpu.VMEM((seg_len, d), F32)])
            conv_p.append(u[t_p - conv_state:t_p].reshape(1, conv_state, d))
            conv_s.append(u[t_p:].reshape(b_s, dec_seq, d)[:, dec_seq - conv_state:])
        else:
            tn_z = _pick(2 * d_gm, (1024, 512, 256, 128))
            z = _norm_matmul(x, norm_mix[i], [bf(gm_w_in[slot])], (0,), 2 * d_gm, _gelu_tanh, BF16,
                             tm, tn_z)
            w_out = bf(gm_w_out[slot])
            tn_g = _pick(d, (512, 256, 128))
            for tile0, n_tiles, blk, emit_v in ((0, n_ptiles, min(seq, GM_BLOCK), False),
                                                (n_ptiles, 1, min(dec_seq, GM_BLOCK), True)):
                ws_big, bias = _gmlp_spatial_weights(gm_w_s[slot], gm_b_s[slot], blk, tm)
                outs = _prologue_matmul(
                    x, w_out, [z, gm_norm[slot].reshape(1, d_gm), ws_big, bias],
                    [pl.BlockSpec((tm, 2 * d_gm), functools.partial(lambda i, j, t0: (i + t0, 0), t0=tile0)),
                     pl.BlockSpec((1, d_gm), lambda i, j: (0, 0)),
                     pl.BlockSpec(ws_big.shape, lambda i, j: (0, 0, 0)),
                     pl.BlockSpec(bias.shape, lambda i, j: (0, 0))],
                    functools.partial(_gmlp_prologue, tm=tm, emit_v=emit_v),
                    tile0=tile0, n_tiles=n_tiles, tm=tm, tn=tn_g,
                    extra_shapes=[jax.ShapeDtypeStruct((tm, d_gm), F32)] if emit_v else [],
                    extra_specs=[pl.BlockSpec((tm, d_gm), lambda i, j: (0, 0))] if emit_v else [],
                    scratch_shapes=[pltpu.VMEM((tm, d_gm), BF16)])
                x = outs[0]
                if emit_v:
                    gm_s.append(outs[1].reshape(b_s, dec_seq, d_gm))

        q = _norm_matmul(x, norm_xattn[i], [bf(xa_w_q[i])], (0,), d, _identity, BF16, tm, tn_d)
        w_o = bf(xa_w_o[i])
        kv = kv_p[i]
        k_p, v_p = bf(kv[:, :d]).reshape(1, mem_len, d), bf(kv[:, d:]).reshape(1, mem_len, d)
        k_s = bf(cache_mem_k[i]).reshape(b_s, mem_len, d)
        v_s = bf(cache_mem_v[i]).reshape(b_s, mem_len, d)
        half = tm // 2
        for tile0, n_tiles, kk, vv, segments in (
                (0, n_ptiles, k_p, v_p, ((0, half, 0), (half, half, 0))),
                (n_ptiles, 1, k_s, v_s, tuple((b * dec_seq, dec_seq, b) for b in range(b_s)))):
            (x,) = _prologue_matmul(
                x, w_o, [q, kk, vv],
                [pl.BlockSpec((tm, d), functools.partial(lambda i, j, t0: (i + t0, 0), t0=tile0)),
                 pl.BlockSpec(kk.shape, lambda i, j: (0, 0, 0)),
                 pl.BlockSpec(vv.shape, lambda i, j: (0, 0, 0))],
                functools.partial(_attn_prologue, segments=segments, n_heads=n_heads),
                tile0=tile0, n_tiles=n_tiles, tm=tm, tn=tn_d)

        j = i // 2
        if i % 2 == 0:
            d_ff = ffn_w1.shape[2]
            tn_f = _pick(d_ff, (512, 256, 128))
            t = _norm_matmul(x, norm_ffn[i], [bf(ffn_w1[j]), bf(ffn_w3[j])], (0, 0), d_ff, _swiglu,
                             BF16, tm, tn_f)
            (x,) = _prologue_matmul(
                x, bf(ffn_w2[j]), [t], [pl.BlockSpec((tm, d_ff), lambda i, j: (i, 0))],
                functools.partial(_copy_prologue, tm=tm),
                tile0=0, n_tiles=t_all // tm, tm=tm, tn=tn_d)
        else:
            x = _moe_ffn(x, norm_ffn[i], moe_router[j], bf(moe_w1[j]), bf(moe_w3[j]),
                         bf(moe_w2[j]), norm_out, tm, final_norm=(i == depth - 1))

    y_prompt = x[:t_p].reshape(1, seq, d)
    y_sample = x[t_p:].reshape(b_s, dec_seq, d)
    return (y_prompt, y_sample, new_mem_k, new_mem_v,
            jnp.stack(pool_tails_p), jnp.stack(pool_tails_s),
            jnp.stack(conv_p), jnp.stack(conv_s), jnp.stack(gm_s))
```

```python
import functools
import math

import jax
import jax.numpy as jnp
from jax import lax
from jax.experimental import pallas as pl
from jax.experimental.pallas import tpu as pltpu

F32 = jnp.float32
BF16 = jnp.bfloat16

EPS = 1e-6
PAST_LEN = 2048
CHUNK = 64
GM_BLOCK = 128
GM_GROUPS = 4
POOL_WINDOWS = (2, 4, 8, 16)
TOP_K = 2

TILE_M = 512
HALO = 32
PAD = 8
ROW_CHUNK = 16
SUBLANES = 8
LANES = 128
VMEM_LIMIT = 60 * 1024 * 1024


def _params(n_grid):
    return pltpu.CompilerParams(
        dimension_semantics=("arbitrary",) * n_grid, vmem_limit_bytes=VMEM_LIMIT)


def _pick(n, prefs):
    for p in prefs:
        if n % p == 0:
            return p
    return n


def _stacked_spec(w, lead, block, index_fn):
    return pl.BlockSpec((None,) * len(lead) + tuple(block),
                        lambda *a: tuple(lead) + tuple(index_fn(*a)))


def _rms_rows(x_ref, g_ref, out_ref, inv_ref, n_rows, x_row0=0, out_row0=0):
    d = x_ref.shape[1]
    n_cc = d // LANES

    def stats(c, carry):
        r = pl.multiple_of(c * ROW_CHUNK, ROW_CHUNK)
        part = jnp.zeros((ROW_CHUNK, LANES), F32)
        for cc in range(n_cc):
            x = x_ref[pl.ds(x_row0 + r, ROW_CHUNK), cc * LANES:(cc + 1) * LANES].astype(F32)
            part = part + x * x
        ms = jnp.sum(part, axis=-1, keepdims=True) / d
        inv_ref[pl.ds(r, ROW_CHUNK), :] = jnp.broadcast_to(lax.rsqrt(ms + EPS), (ROW_CHUNK, LANES))
        return carry
    lax.fori_loop(0, n_rows // ROW_CHUNK, stats, 0, unroll=4)

    def scale(c, carry):
        r = pl.multiple_of(c * ROW_CHUNK, ROW_CHUNK)
        inv = inv_ref[pl.ds(r, ROW_CHUNK), :]
        for cc in range(n_cc):
            cols = slice(cc * LANES, (cc + 1) * LANES)
            x = x_ref[pl.ds(x_row0 + r, ROW_CHUNK), cols].astype(F32)
            out_ref[pl.ds(out_row0 + r, ROW_CHUNK), cols] = (
                (x * inv) * g_ref[:, cols]).astype(out_ref.dtype)
        return carry
    lax.fori_loop(0, n_rows // ROW_CHUNK, scale, 0, unroll=2)


def _cast_rows(x_ref, out_ref, n_rows):
    def body(c, carry):
        r = pl.multiple_of(c * ROW_CHUNK, ROW_CHUNK)
        out_ref[pl.ds(r, ROW_CHUNK), :] = x_ref[pl.ds(r, ROW_CHUNK), :].astype(out_ref.dtype)
        return carry
    lax.fori_loop(0, n_rows // ROW_CHUNK, body, 0, unroll=4)


def _sigmoid(x):
    return 1.0 / (1.0 + jnp.exp(-x))


def _silu(x):
    return x * _sigmoid(x)


def _gelu_tanh(x):
    c = math.sqrt(2.0 / math.pi)
    return 0.5 * x * (1.0 + jnp.tanh(c * (x + 0.044715 * (x * x * x))))


def _glu(a, gate):
    return a * _sigmoid(gate)


def _swiglu(a, b):
    return _silu(a) * b


def _identity(a):
    return a


def _norm_matmul_kernel(*refs, n_w, epilogue, tm):
    x_ref, g_ref = refs[0], refs[1]
    w_refs = refs[2:2 + n_w]
    o_ref = refs[2 + n_w]
    h_ref, inv_ref = refs[3 + n_w], refs[4 + n_w]

    @pl.when(pl.program_id(1) == 0)
    def _():
        _rms_rows(x_ref, g_ref, h_ref, inv_ref, tm)

    h = h_ref[...]
    accs = [jnp.dot(h, w[...], preferred_element_type=F32) for w in w_refs]
    o_ref[...] = epilogue(*accs).astype(o_ref.dtype)


def _norm_matmul(x, g, ws, n_out, epilogue, out_dtype, tm, tn):
    m, k = x.shape
    grid = (m // tm, n_out // tn)
    w_specs = [_stacked_spec(w, lead, (k, tn), functools.partial(lambda i, j, c0: (0, j + c0), c0=c0))
               for w, lead, c0 in ws]
    return pl.pallas_call(
        functools.partial(_norm_matmul_kernel, n_w=len(ws), epilogue=epilogue, tm=tm),
        grid=grid,
        in_specs=[pl.BlockSpec((tm, k), lambda i, j: (i, 0)),
                  pl.BlockSpec((1, k), lambda i, j: (0, 0))] + w_specs,
        out_specs=pl.BlockSpec((tm, tn), lambda i, j: (i, j)),
        out_shape=jax.ShapeDtypeStruct((m, n_out), out_dtype),
        scratch_shapes=[pltpu.VMEM((tm, k), BF16), pltpu.VMEM((tm, LANES), F32)],
        compiler_params=_params(2),
        name="norm_matmul" + epilogue.__name__,
    )(x, g.reshape(1, k), *[w for w, _, _ in ws])


def _prologue_matmul_kernel(*refs, n_pro, n_extra, prologue):
    pro_refs = refs[:n_pro]
    w_ref, res_ref, o_ref = refs[n_pro], refs[n_pro + 1], refs[n_pro + 2]
    extra_refs = refs[n_pro + 3:n_pro + 3 + n_extra]
    scratch = refs[n_pro + 3 + n_extra:]
    if prologue is None:
        a_ref = pro_refs[0]
    else:
        a_ref = scratch[0]

        @pl.when(pl.program_id(1) == 0)
        def _():
            prologue(pro_refs, extra_refs, a_ref, scratch[1:])

    o_ref[...] = res_ref[...] + jnp.dot(a_ref[...], w_ref[...], preferred_element_type=F32)


def _prologue_matmul(x_res, w, w_lead, pro_inputs, pro_specs, prologue, *, name, tile0, n_tiles,
                     tm, tn, extra_shapes=(), extra_specs=(), scratch_shapes=()):
    k, n = w.shape[-2:]
    n_pro = len(pro_inputs)
    a_scratch = [] if prologue is None else [pltpu.VMEM((tm, k), BF16)]
    return pl.pallas_call(
        functools.partial(_prologue_matmul_kernel, n_pro=n_pro, n_extra=len(extra_shapes),
                          prologue=prologue),
        grid=(n_tiles, n // tn),
        in_specs=list(pro_specs) + [
            _stacked_spec(w, w_lead, (k, tn), lambda i, j: (0, j)),
            pl.BlockSpec((tm, tn), lambda i, j: (i + tile0, j))],
        out_specs=[pl.BlockSpec((tm, tn), lambda i, j: (i + tile0, j))] + list(extra_specs),
        out_shape=[jax.ShapeDtypeStruct(x_res.shape, x_res.dtype)] + list(extra_shapes),
        scratch_shapes=a_scratch + list(scratch_shapes),
        input_output_aliases={n_pro + 1: 0},
        compiler_params=_params(2),
        name=name,
    )(*pro_inputs, w, x_res)


def _tile_spec(tm, width, tile0):
    return pl.BlockSpec((tm, width), lambda i, j: (i + tile0, 0))


def _full_spec(shape):
    return pl.BlockSpec(tuple(shape), lambda i, j: (0,) * len(shape))


def _pool_kernel(x_ref, g_ref, st_ref, w_ref, sc_ref, o_ref, tail_ref, hp_ref, sa_ref, sb_ref,
                 d_ref, inv_ref, *, tm, seg_len, carry, start_pos):
    d_model = x_ref.shape[1]
    n_groups = len(POOL_WINDOWS)
    gw = d_model // n_groups
    base = PAD + HALO
    n_seg = tm // seg_len
    i = pl.program_id(0)

    hp_ref[0:PAD, :] = jnp.zeros((PAD, d_model), F32)
    sa_ref[0:PAD, :] = jnp.zeros((PAD, gw), F32)
    sb_ref[0:PAD, :] = jnp.zeros((PAD, gw), F32)

    for s in range(n_seg):
        row0 = s * seg_len
        if carry:
            @pl.when(i == 0)
            def _():
                hp_ref[PAD:base, :] = jnp.zeros((HALO, d_model), F32)

            @pl.when(i > 0)
            def _():
                hp_ref[PAD:base, :] = hp_ref[PAD + seg_len:base + seg_len, :]
        else:
            hp_ref[PAD:base, :] = st_ref[s]

        _rms_rows(x_ref, g_ref, hp_ref, inv_ref, seg_len, x_row0=row0, out_row0=base)
        tail_ref[s] = hp_ref[base + seg_len - 16:base + seg_len, :]

        if carry:
            pos0 = start_pos + i * tm + row0
        else:
            pos0 = start_pos
        pos = (pos0 + lax.broadcasted_iota(jnp.int32, (seg_len, 1), 0)).astype(F32)

        n_steps = (HALO + seg_len) // 32
        for g, win in enumerate(POOL_WINDOWS):
            c0 = g * gw
            bufs = (sa_ref, sb_ref)
            n_stage = int(math.log2(win))
            for t in range(n_stage):
                shift = 1 << t
                dst = bufs[t % 2]
                for c in range(n_steps):
                    r = PAD + c * 32
                    if t == 0:
                        a = hp_ref[r:r + 32, c0:c0 + gw]
                        b = hp_ref[r - shift:r - shift + 32, c0:c0 + gw]
                    else:
                        src = bufs[(t - 1) % 2]
                        a = src[r:r + 32, :]
                        b = src[r - shift:r - shift + 32, :]
                    dst[r:r + 32, :] = a + b
            sums = bufs[(n_stage - 1) % 2]
            cnt = jnp.minimum(pos + 1.0, float(win))
            mean = sums[base:base + seg_len, :] / cnt
            d_ref[row0:row0 + seg_len, c0:c0 + gw] = (
                mean - hp_ref[base:base + seg_len, c0:c0 + gw]).astype(BF16)

    for g in range(n_groups):
        c0 = g * gw
        mix = jnp.dot(d_ref[:, c0:c0 + gw], w_ref[g], preferred_element_type=F32)
        o_ref[:, c0:c0 + gw] = x_ref[:, c0:c0 + gw] + mix * sc_ref[:, c0:c0 + gw]


def _pool_mix(x_all, g, state, w_bf, slot, scale, *, tile0, n_tiles, tm, seg_len, carry, start_pos):
    t_all, d = x_all.shape
    n_seg = tm // seg_len
    gw = d // len(POOL_WINDOWS)
    return pl.pallas_call(
        functools.partial(_pool_kernel, tm=tm, seg_len=seg_len, carry=carry, start_pos=start_pos),
        grid=(n_tiles,),
        in_specs=[pl.BlockSpec((tm, d), lambda i: (i + tile0, 0)),
                  pl.BlockSpec((1, d), lambda i: (0, 0)),
                  pl.BlockSpec(state.shape, lambda i: (0, 0, 0)),
                  _stacked_spec(w_bf, (slot,), w_bf.shape[1:], lambda i: (0, 0, 0)),
                  pl.BlockSpec((1, d), lambda i: (0, 0))],
        out_specs=[pl.BlockSpec((tm, d), lambda i: (i + tile0, 0)),
                   pl.BlockSpec((n_seg, 16, d), lambda i: (0, 0, 0))],
        out_shape=[jax.ShapeDtypeStruct((t_all, d), F32),
                   jax.ShapeDtypeStruct((n_seg, 16, d), F32)],
        scratch_shapes=[pltpu.VMEM((PAD + HALO + seg_len, d), F32),
                        pltpu.VMEM((PAD + HALO + seg_len, gw), F32),
                        pltpu.VMEM((PAD + HALO + seg_len, gw), F32),
                        pltpu.VMEM((tm, d), BF16),
                        pltpu.VMEM((seg_len, LANES), F32)],
        input_output_aliases={0: 0},
        compiler_params=_params(1),
        name="pool_mix_carry" if carry else "pool_mix_state",
    )(x_all, g.reshape(1, d), state, w_bf, scale.reshape(1, d))


CONV_STRIDE = 4


def _conv_prologue(pro_refs, extra_refs, a_ref, scratch, *, tm, seg_len, carry, width):
    u_ref, st_ref, wb_ref, bb_ref, gn_ref = pro_refs
    up_ref, c_ref = scratch
    d_model = u_ref.shape[1]
    n_slabs = d_model // LANES
    n_seg = tm // seg_len
    first = HALO - (width - 1)
    rows_per_step = SUBLANES * CONV_STRIDE
    i = pl.program_id(0)

    for s in range(n_seg):
        row0 = s * seg_len
        if carry:
            @pl.when(i == 0)
            def _():
                up_ref[:, 0:HALO, :] = jnp.zeros((n_slabs, HALO, LANES), F32)

            @pl.when(i > 0)
            def _():
                up_ref[:, 0:HALO, :] = up_ref[:, seg_len:seg_len + HALO, :]
        else:
            up_ref[:, 0:HALO, :] = st_ref[s]
        for sl in range(n_slabs):
            up_ref[sl, HALO:HALO + seg_len, :] = u_ref[row0:row0 + seg_len, sl * LANES:(sl + 1) * LANES]

        def slab_body(sl, cr):
            def step_body(c, cr2):
                r0 = pl.multiple_of(c * rows_per_step, rows_per_step)
                accs = [[bb_ref[sl], jnp.zeros((SUBLANES, LANES), F32)] for _ in range(CONV_STRIDE)]
                for m in range(width + CONV_STRIDE - 1):
                    x = up_ref[sl, pl.ds(r0 + first + m, SUBLANES, stride=CONV_STRIDE), :]
                    for ph in range(CONV_STRIDE):
                        k = m - ph
                        if 0 <= k < width:
                            accs[ph][k % 2] = accs[ph][k % 2] + (
                                x * wb_ref[sl, SUBLANES * k:SUBLANES * (k + 1), :])
                for ph in range(CONV_STRIDE):
                    c_ref[sl, pl.ds(r0 + ph, SUBLANES, stride=CONV_STRIDE), :] = accs[ph][0] + accs[ph][1]
                return cr2
            lax.fori_loop(0, seg_len // rows_per_step, step_body, 0, unroll=2)
            return cr
        lax.fori_loop(0, n_slabs, slab_body, 0)

        def norm_body(c, cr):
            r = pl.multiple_of(c * ROW_CHUNK, ROW_CHUNK)
            part = jnp.zeros((ROW_CHUNK, LANES), F32)
            for sl in range(n_slabs):
                v = c_ref[sl, pl.ds(r, ROW_CHUNK), :]
                part = part + v * v
            inv = lax.rsqrt(jnp.sum(part, axis=-1, keepdims=True) / d_model + EPS)
            for sl in range(n_slabs):
                cols = slice(sl * LANES, (sl + 1) * LANES)
                y = (c_ref[sl, pl.ds(r, ROW_CHUNK), :] * inv) * gn_ref[:, cols]
                a_ref[pl.ds(row0 + r, ROW_CHUNK), cols] = _silu(y).astype(BF16)
            return cr
        lax.fori_loop(0, seg_len // ROW_CHUNK, norm_body, 0, unroll=2)


def _to_slabs(a):
    *lead, rows, d = a.shape
    a = a.reshape(*lead, rows, d // LANES, LANES)
    return jnp.swapaxes(a, -3, -2)


GM_COLS = 512


def _gmlp_prologue(pro_refs, extra_refs, a_ref, scratch, *, tm, emit_v):
    z_ref, gv_ref, ws_ref, bs_ref = pro_refs
    (vn_ref,) = scratch
    d_gm = gv_ref.shape[1]
    gcols = d_gm // GM_GROUPS
    n_cc = d_gm // 1024

    def norm_body(c, cr):
        r = pl.multiple_of(c * ROW_CHUNK, ROW_CHUNK)
        ssq = jnp.zeros((ROW_CHUNK, 1), F32)
        for cc in range(n_cc):
            v = z_ref[pl.ds(r, ROW_CHUNK), d_gm + cc * 1024:d_gm + (cc + 1) * 1024].astype(F32)
            ssq = ssq + jnp.sum(v * v, axis=-1, keepdims=True)
        inv = lax.rsqrt(ssq / d_gm + EPS)
        for cc in range(n_cc):
            cols = slice(cc * 1024, (cc + 1) * 1024)
            v = z_ref[pl.ds(r, ROW_CHUNK), d_gm + cc * 1024:d_gm + (cc + 1) * 1024].astype(F32)
            vn = (v * inv) * gv_ref[:, cols]
            vn_ref[pl.ds(r, ROW_CHUNK), cols] = vn.astype(BF16)
            if emit_v:
                extra_refs[0][pl.ds(r, ROW_CHUNK), cols] = vn
        return cr
    lax.fori_loop(0, tm // ROW_CHUNK, norm_body, 0, unroll=2)

    for g in range(GM_GROUPS):
        for cc in range(gcols // GM_COLS):
            cols = slice(g * gcols + cc * GM_COLS, g * gcols + (cc + 1) * GM_COLS)
            s = jnp.dot(ws_ref[g], vn_ref[:, cols], preferred_element_type=F32)
            s = s + bs_ref[:, g:g + 1]
            a_ref[:, cols] = (z_ref[:, cols].astype(F32) * s).astype(BF16)


def _gmlp_spatial_weights(w_s, b_s, blk, tm):
    cidx = jnp.arange(blk) // CHUNK
    mask = cidx[None, :] <= cidx[:, None]
    ws = jnp.where(mask[None], w_s[:, :blk, :blk], 0.0)
    eye = jnp.eye(tm // blk, dtype=ws.dtype)
    big = jnp.einsum("ab,gij->gaibj", eye, ws).reshape(w_s.shape[0], tm, tm)
    bias = jnp.tile(b_s[:, :blk].T, (tm // blk, 1))
    return big.astype(BF16), bias.astype(F32)


def _attn_prologue(pro_refs, extra_refs, a_ref, scratch, *, segments, n_heads):
    q_ref, k_ref, v_ref = pro_refs
    d_model = q_ref.shape[1]
    hd = d_model // n_heads
    scale = hd ** -0.5
    for row0, n_rows, kv in segments:
        for h in range(n_heads):
            cols = slice(h * hd, (h + 1) * hd)
            q = q_ref[row0:row0 + n_rows, cols]
            s = lax.dot_general(q, k_ref[kv, :, cols], (((1,), (1,)), ((), ())),
                                preferred_element_type=F32) * scale
            m = jnp.max(s, axis=-1, keepdims=True)
            e = jnp.exp(s - m)
            p = e / jnp.sum(e, axis=-1, keepdims=True)
            o = jnp.dot(p.astype(BF16), v_ref[kv, :, cols], preferred_element_type=F32)
            a_ref[row0:row0 + n_rows, cols] = o.astype(BF16)


def _router_kernel(x_ref, g_ref, wr_ref, h_ref, meta_ref, gate_ref, cnt_ref, run_ref, inv_ref, *,
                   tm, n_experts):
    i = pl.program_id(0)

    @pl.when(i == 0)
    def _():
        run_ref[...] = jnp.zeros(run_ref.shape, F32)

    _rms_rows(x_ref, g_ref, h_ref, inv_ref, tm)
    logits = jnp.dot(h_ref[...], wr_ref[...], preferred_element_type=F32,
                     precision=lax.Precision.HIGHEST)
    lane = lax.broadcasted_iota(jnp.int32, (tm, LANES), 1)
    neg = jnp.float32(-jnp.inf)
    logits = jnp.where(lane < n_experts, logits, neg)
    m1 = jnp.max(logits, axis=-1, keepdims=True)
    i1 = jnp.min(jnp.where(logits == m1, lane, LANES), axis=-1, keepdims=True)
    rest = jnp.where(lane == i1, neg, logits)
    m2 = jnp.max(rest, axis=-1, keepdims=True)
    i2 = jnp.min(jnp.where(rest == m2, lane, LANES), axis=-1, keepdims=True)
    e2 = jnp.exp(m2 - m1)
    den = 1.0 + e2
    g1 = 1.0 / den
    g2 = e2 / den

    sel1 = lane == i1
    sel2 = lane == i2
    onehot = jnp.where(sel1 | sel2, 1.0, 0.0)
    r_i = lax.broadcasted_iota(jnp.int32, (tm, tm), 0)
    c_i = lax.broadcasted_iota(jnp.int32, (tm, tm), 1)
    tri = jnp.where(c_i < r_i, 1.0, 0.0).astype(BF16)
    before = jnp.dot(tri, onehot.astype(BF16), preferred_element_type=F32) + run_ref[0:1, :]
    rank1 = jnp.sum(jnp.where(sel1, before, 0.0), axis=-1, keepdims=True)
    rank2 = jnp.sum(jnp.where(sel2, before, 0.0), axis=-1, keepdims=True)
    run_ref[0:1, :] = run_ref[0:1, :] + jnp.sum(onehot, axis=0, keepdims=True)
    cnt_ref[...] = jnp.broadcast_to(run_ref[0:1, :], cnt_ref.shape)

    col = lax.broadcasted_iota(jnp.int32, (tm, 8), 1)
    meta_ref[...] = jnp.where(col == 0, i1.astype(F32),
                    jnp.where(col == 1, i2.astype(F32),
                    jnp.where(col == 2, rank1,
                    jnp.where(col == 3, rank2, 0.0))))
    gate_ref[:, 0:LANES] = jnp.broadcast_to(g1, (tm, LANES))
    gate_ref[:, LANES:2 * LANES] = jnp.broadcast_to(g2, (tm, LANES))


def _router(x_all, g, w_r, tm):
    t_all, d = x_all.shape
    n_experts = w_r.shape[1]
    wr_pad = jnp.zeros((d, LANES), F32).at[:, :n_experts].set(w_r)
    return pl.pallas_call(
        functools.partial(_router_kernel, tm=tm, n_experts=n_experts),
        grid=(t_all // tm,),
        in_specs=[pl.BlockSpec((tm, d), lambda i: (i, 0)),
                  pl.BlockSpec((1, d), lambda i: (0, 0)),
                  pl.BlockSpec((d, LANES), lambda i: (0, 0))],
        out_specs=[pl.BlockSpec((tm, d), lambda i: (i, 0)),
                   pl.BlockSpec((tm, 8), lambda i: (i, 0)),
                   pl.BlockSpec((tm, 2 * LANES), lambda i: (i, 0)),
                   pl.BlockSpec((8, LANES), lambda i: (0, 0))],
        out_shape=[jax.ShapeDtypeStruct((t_all, d), F32),
                   jax.ShapeDtypeStruct((t_all, 8), F32),
                   jax.ShapeDtypeStruct((t_all, 2 * LANES), F32),
                   jax.ShapeDtypeStruct((8, LANES), F32)],
        scratch_shapes=[pltpu.VMEM((8, LANES), F32), pltpu.VMEM((tm, LANES), F32)],
        compiler_params=_params(1),
        name="moe_router",
    )(x_all, g.reshape(1, d), wr_pad)


def _dispatch_kernel(p1_ref, p2_ref, h_ref, buf_in_ref, buf_ref, sem, *, tm):
    del buf_in_ref
    t0 = pl.program_id(0) * tm

    def copies(r):
        t = t0 + r
        return (pltpu.make_async_copy(h_ref.at[pl.ds(r, 1)], buf_ref.at[pl.ds(p1_ref[t], 1)], sem),
                pltpu.make_async_copy(h_ref.at[pl.ds(r, 1)], buf_ref.at[pl.ds(p2_ref[t], 1)], sem))

    def start(r, cr):
        for c in copies(r):
            c.start()
        return cr
    lax.fori_loop(0, tm, start, 0, unroll=8)

    def wait(r, cr):
        for c in copies(r):
            c.wait()
        return cr
    lax.fori_loop(0, tm, wait, 0, unroll=8)


def _dispatch(h, p1, p2, n_slots, tm):
    t_all, d = h.shape
    buf0 = jnp.zeros((n_slots, d), h.dtype)
    return pl.pallas_call(
        functools.partial(_dispatch_kernel, tm=tm),
        grid_spec=pltpu.PrefetchScalarGridSpec(
            num_scalar_prefetch=2,
            grid=(t_all // tm,),
            in_specs=[pl.BlockSpec((tm, d), lambda i, a, b: (i, 0)),
                      pl.BlockSpec(memory_space=pl.ANY)],
            out_specs=pl.BlockSpec(memory_space=pl.ANY),
            scratch_shapes=[pltpu.SemaphoreType.DMA(())]),
        out_shape=jax.ShapeDtypeStruct((n_slots, d), h.dtype),
        input_output_aliases={3: 0},
        compiler_params=_params(1),
        name="moe_dispatch",
    )(p1, p2, h, buf0)


def _expert_up_kernel(te_ref, nu_ref, x_ref, w1_ref, w3_ref, o_ref, h_ref, *, tm):
    i = pl.program_id(0)

    @pl.when(i < nu_ref[0])
    def _():
        @pl.when(pl.program_id(1) == 0)
        def _():
            _cast_rows(x_ref, h_ref, tm)
        h = h_ref[...]
        a = jnp.dot(h, w1_ref[...], preferred_element_type=F32)
        b = jnp.dot(h, w3_ref[...], preferred_element_type=F32)
        o_ref[...] = _swiglu(a, b).astype(o_ref.dtype)

    @pl.when(i >= nu_ref[0])
    def _():
        o_ref[...] = jnp.zeros(o_ref.shape, o_ref.dtype)


def _expert_up(buf, w1, w3, layer, tile_expert, n_used, tm, tn):
    n_slots, d = buf.shape
    f = w1.shape[-1]
    w_spec = lambda: pl.BlockSpec((None, None, d, tn), lambda i, j, te, nu: (layer, te[i], 0, j))
    return pl.pallas_call(
        functools.partial(_expert_up_kernel, tm=tm),
        grid_spec=pltpu.PrefetchScalarGridSpec(
            num_scalar_prefetch=2,
            grid=(n_slots // tm, f // tn),
            in_specs=[pl.BlockSpec((tm, d), lambda i, j, te, nu: (i, 0)), w_spec(), w_spec()],
            out_specs=pl.BlockSpec((tm, tn), lambda i, j, te, nu: (i, j)),
            scratch_shapes=[pltpu.VMEM((tm, d), BF16)]),
        out_shape=jax.ShapeDtypeStruct((n_slots, f), BF16),
        compiler_params=_params(2),
        name="moe_expert_up",
    )(tile_expert, n_used, buf, w1, w3)


def _expert_down_kernel(te_ref, nu_ref, x_ref, w_ref, o_ref):
    i = pl.program_id(0)

    @pl.when(i < nu_ref[0])
    def _():
        o_ref[...] = jnp.dot(x_ref[...], w_ref[...], preferred_element_type=F32)

    @pl.when(i >= nu_ref[0])
    def _():
        o_ref[...] = jnp.zeros(o_ref.shape, o_ref.dtype)


def _expert_down(t_sorted, w2, layer, tile_expert, n_used, tm, tn):
    n_slots, f = t_sorted.shape
    d = w2.shape[-1]
    return pl.pallas_call(
        _expert_down_kernel,
        grid_spec=pltpu.PrefetchScalarGridSpec(
            num_scalar_prefetch=2,
            grid=(n_slots // tm, d // tn),
            in_specs=[pl.BlockSpec((tm, f), lambda i, j, te, nu: (i, 0)),
                      pl.BlockSpec((None, None, f, tn), lambda i, j, te, nu: (layer, te[i], 0, j))],
            out_specs=pl.BlockSpec((tm, tn), lambda i, j, te, nu: (i, j))),
        out_shape=jax.ShapeDtypeStruct((n_slots, d), F32),
        compiler_params=_params(2),
        name="moe_expert_down",
    )(tile_expert, n_used, t_sorted, w2)


def _combine_kernel(p1_ref, p2_ref, x_ref, gate_ref, y_ref, gout_ref, o_ref, a_ref, b_ref, sem, *,
                    tm, final_norm):
    t0 = pl.program_id(0) * tm

    def copies(r):
        t = t0 + r
        return (pltpu.make_async_copy(y_ref.at[pl.ds(p1_ref[t], 1)], a_ref.at[pl.ds(r, 1)], sem),
                pltpu.make_async_copy(y_ref.at[pl.ds(p2_ref[t], 1)], b_ref.at[pl.ds(r, 1)], sem))

    def start(r, cr):
        for c in copies(r):
            c.start()
        return cr
    lax.fori_loop(0, tm, start, 0, unroll=8)

    def wait(r, cr):
        for c in copies(r):
            c.wait()
        return cr
    lax.fori_loop(0, tm, wait, 0, unroll=8)

    d = x_ref.shape[1]
    n_cc = d // LANES

    def body(c, cr):
        r = pl.multiple_of(c * ROW_CHUNK, ROW_CHUNK)
        rows = pl.ds(r, ROW_CHUNK)
        g1 = gate_ref[rows, 0:LANES]
        g2 = gate_ref[rows, LANES:2 * LANES]
        part = jnp.zeros((ROW_CHUNK, LANES), F32)
        for cc in range(n_cc):
            cols = slice(cc * LANES, (cc + 1) * LANES)
            x = x_ref[rows, cols] + (g1 * a_ref[rows, cols] + g2 * b_ref[rows, cols])
            o_ref[rows, cols] = x
            if final_norm:
                part = part + x * x
        if final_norm:
            ms = jnp.sum(part, axis=-1, keepdims=True) / d
            inv = jnp.broadcast_to(lax.rsqrt(ms + EPS), (ROW_CHUNK, LANES))
            for cc in range(n_cc):
                cols = slice(cc * LANES, (cc + 1) * LANES)
                o_ref[rows, cols] = (o_ref[rows, cols] * inv) * gout_ref[:, cols]
        return cr
    lax.fori_loop(0, tm // ROW_CHUNK, body, 0, unroll=2)


def _combine(x_all, gates, y_sorted, p1, p2, g_out, tm, final_norm):
    t_all, d = x_all.shape
    return pl.pallas_call(
        functools.partial(_combine_kernel, tm=tm, final_norm=final_norm),
        grid_spec=pltpu.PrefetchScalarGridSpec(
            num_scalar_prefetch=2,
            grid=(t_all // tm,),
            in_specs=[pl.BlockSpec((tm, d), lambda i, a, b: (i, 0)),
                      pl.BlockSpec((tm, 2 * LANES), lambda i, a, b: (i, 0)),
                      pl.BlockSpec(memory_space=pl.ANY),
                      pl.BlockSpec((1, d), lambda i, a, b: (0, 0))],
            out_specs=pl.BlockSpec((tm, d), lambda i, a, b: (i, 0)),
            scratch_shapes=[pltpu.VMEM((tm, d), F32), pltpu.VMEM((tm, d), F32),
                            pltpu.SemaphoreType.DMA(())]),
        out_shape=jax.ShapeDtypeStruct((t_all, d), F32),
        input_output_aliases={2: 0},
        compiler_params=_params(1),
        name="moe_combine",
    )(p1, p2, x_all, gates, y_sorted, g_out.reshape(1, d))


def _moe_ffn(x_all, g, w_r, w1, w3, w2, layer, g_out, tm, final_norm):
    t_all, d = x_all.shape
    n_experts = w_r.shape[1]
    h, meta, gates, counts = _router(x_all, g, w_r, tm)

    cnt = counts[0, :n_experts].astype(jnp.int32)
    padded = ((cnt + tm - 1) // tm) * tm
    ends = jnp.cumsum(padded)
    offs = ends - padded
    e1 = meta[:, 0].astype(jnp.int32)
    e2 = meta[:, 1].astype(jnp.int32)
    p1 = offs[e1] + meta[:, 2].astype(jnp.int32)
    p2 = offs[e2] + meta[:, 3].astype(jnp.int32)
    n_tiles = (TOP_K * t_all) // tm + n_experts
    tile_start = jnp.arange(n_tiles, dtype=jnp.int32) * tm
    tile_expert = jnp.minimum(
        jnp.sum((tile_start[:, None] >= ends[None, :]).astype(jnp.int32), axis=1), n_experts - 1)
    n_used = (ends[-1] // tm).reshape(1).astype(jnp.int32)

    buf = _dispatch(h, p1, p2, n_tiles * tm, tm)
    f = w1.shape[-1]
    t_sorted = _expert_up(buf, w1, w3, layer, tile_expert, n_used, tm,
                          _pick(f, (1408, 512, 256, 128)))
    y_sorted = _expert_down(t_sorted, w2, layer, tile_expert, n_used, tm,
                            _pick(d, (1024, 512, 256, 128)))
    return _combine(x_all, gates, y_sorted, p1, p2, g_out, tm, final_norm)


def kernel(x_prompt, x_sample, cache_mem_k, cache_mem_v, state_pool, state_conv, mem_prompt,
           norm_mix, norm_xattn, norm_mem, norm_ffn, norm_out, pool_w, pool_scale,
           conv_w_in, conv_w_dw, conv_b_dw, conv_norm, conv_w_out,
           gm_w_in, gm_norm, gm_w_s, gm_b_s, gm_w_out, xa_w_q, xa_w_kv, xa_w_o,
           ffn_w1, ffn_w3, ffn_w2, moe_router, moe_w1, moe_w3, moe_w2):
    b_p, seq, d = x_prompt.shape
    b_s, dec_seq, _ = x_sample.shape
    depth = norm_mix.shape[0]
    mem_len = mem_prompt.shape[1]
    n_heads = cache_mem_k.shape[3]
    pool_state = state_pool.shape[2]
    conv_width = conv_w_dw.shape[1]
    conv_state = state_conv.shape[2]
    d_gm = gm_norm.shape[1]
    d_ff = ffn_w1.shape[2]
    tm = TILE_M
    assert b_p == 1 and seq % tm == 0 and b_s * dec_seq == tm
    assert conv_state <= HALO and pool_state <= 16 and seq >= conv_state
    assert dec_seq % (SUBLANES * CONV_STRIDE) == 0 and d % LANES == 0
    assert depth % 2 == 0, "the output norm is fused into the last layer's expert FFN"
    n_ptiles = seq // tm
    t_p = seq
    t_all = seq + tm

    x = jnp.concatenate([x_prompt.reshape(seq, d), x_sample.reshape(tm, d)], axis=0)
    bf = lambda a: a.astype(BF16)
    pool_w, conv_w_in, conv_w_out, gm_w_in, gm_w_out = map(
        bf, (pool_w, conv_w_in, conv_w_out, gm_w_in, gm_w_out))
    xa_w_q, xa_w_kv, xa_w_o, ffn_w1, ffn_w3, ffn_w2, moe_w1, moe_w3, moe_w2 = map(
        bf, (xa_w_q, xa_w_kv, xa_w_o, ffn_w1, ffn_w3, ffn_w2, moe_w1, moe_w3, moe_w2))

    tn_d = _pick(d, (1024, 512, 256, 128))

    mem = mem_prompt.reshape(mem_len, d)
    tn_kv = _pick(2 * d, (1024, 512, 256, 128))
    kv_p = [_norm_matmul(mem, norm_mem[i], [(xa_w_kv, (i,), 0)], 2 * d, _identity, F32, mem_len, tn_kv)
            for i in range(depth)]
    new_mem_k = jnp.stack([kv[:, :d] for kv in kv_p]).reshape(depth, 1, mem_len, n_heads, d // n_heads)
    new_mem_v = jnp.stack([kv[:, d:] for kv in kv_p]).reshape(depth, 1, mem_len, n_heads, d // n_heads)

    pool_tails_p, pool_tails_s, conv_p, conv_s, gm_s = [], [], [], [], []
    for i in range(depth):
        slot = i // 3
        mixer = i % 3
        if mixer == 0:
            zero_state = jnp.zeros((1, HALO, d), F32)
            x, tail_p = _pool_mix(x, norm_mix[i], zero_state, pool_w, slot, pool_scale[slot], tile0=0,
                                  n_tiles=n_ptiles, tm=tm, seg_len=tm, carry=True, start_pos=0)
            st = jnp.zeros((b_s, HALO, d), F32).at[:, HALO - pool_state:].set(state_pool[slot])
            x, tail_s = _pool_mix(x, norm_mix[i], st, pool_w, slot, pool_scale[slot], tile0=n_ptiles,
                                  n_tiles=1, tm=tm, seg_len=dec_seq, carry=False,
                                  start_pos=PAST_LEN)
            pool_tails_p.append(tail_p[:, 16 - pool_state:])
            pool_tails_s.append(tail_s[:, 16 - pool_state:])
        elif mixer == 1:
            u = _norm_matmul(x, norm_mix[i], [(conv_w_in, (slot,), 0), (conv_w_in, (slot,), d // tn_d)],
                             d, _glu, F32, tm, tn_d)
            n_slabs = d // LANES
            wb = _to_slabs(jnp.repeat(conv_w_dw[slot], SUBLANES, axis=0))
            bb = _to_slabs(jnp.broadcast_to(conv_b_dw[slot][None], (SUBLANES, d)))
            small = [wb, bb, conv_norm[slot].reshape(1, d)]
            small_specs = [_full_spec(wb.shape), _full_spec(bb.shape), _full_spec((1, d))]
            st_s = jnp.zeros((b_s, HALO, d), F32).at[:, HALO - conv_state:].set(state_conv[slot])
            for tile0, n_tiles, seg_len, carry, st in (
                    (0, n_ptiles, tm, True, jnp.zeros((1, n_slabs, HALO, LANES), F32)),
                    (n_ptiles, 1, dec_seq, False, _to_slabs(st_s))):
                (x,) = _prologue_matmul(
                    x, conv_w_out, (slot,), [u, st] + small,
                    [_tile_spec(tm, d, tile0), _full_spec(st.shape)] + small_specs,
                    functools.partial(_conv_prologue, tm=tm, seg_len=seg_len, carry=carry,
                                      width=conv_width),
                    name="conv_out_carry" if carry else "conv_out_state",
                    tile0=tile0, n_tiles=n_tiles, tm=tm, tn=tn_d,
                    scratch_shapes=[pltpu.VMEM((n_slabs, HALO + seg_len, LANES), F32),
                                    pltpu.VMEM((n_slabs, seg_len, LANES), F32)])
            conv_p.append(u[t_p - conv_state:t_p].reshape(1, conv_state, d))
            conv_s.append(u[t_p:].reshape(b_s, dec_seq, d)[:, dec_seq - conv_state:])
        else:
            tn_z = _pick(2 * d_gm, (1024, 512, 256, 128))
            z = _norm_matmul(x, norm_mix[i], [(gm_w_in, (slot,), 0)], 2 * d_gm, _gelu_tanh, BF16,
                             tm, tn_z)
            tn_g = _pick(d, (512, 256, 128))
            for tile0, n_tiles, blk, emit_v in ((0, n_ptiles, min(seq, GM_BLOCK), False),
                                                (n_ptiles, 1, min(dec_seq, GM_BLOCK), True)):
                ws_big, bias = _gmlp_spatial_weights(gm_w_s[slot], gm_b_s[slot], blk, tm)
                outs = _prologue_matmul(
                    x, gm_w_out, (slot,), [z, gm_norm[slot].reshape(1, d_gm), ws_big, bias],
                    [_tile_spec(tm, 2 * d_gm, tile0), _full_spec((1, d_gm)),
                     _full_spec(ws_big.shape), _full_spec(bias.shape)],
                    functools.partial(_gmlp_prologue, tm=tm, emit_v=emit_v),
                    name="gmlp_out_state" if emit_v else "gmlp_out",
                    tile0=tile0, n_tiles=n_tiles, tm=tm, tn=tn_g,
                    extra_shapes=[jax.ShapeDtypeStruct((tm, d_gm), F32)] if emit_v else [],
                    extra_specs=[_full_spec((tm, d_gm))] if emit_v else [],
                    scratch_shapes=[pltpu.VMEM((tm, d_gm), BF16)])
                x = outs[0]
                if emit_v:
                    gm_s.append(outs[1].reshape(b_s, dec_seq, d_gm))

        q = _norm_matmul(x, norm_xattn[i], [(xa_w_q, (i,), 0)], d, _identity, BF16, tm, tn_d)
        kv = kv_p[i]
        k_p, v_p = bf(kv[:, :d]).reshape(1, mem_len, d), bf(kv[:, d:]).reshape(1, mem_len, d)
        k_s = bf(cache_mem_k[i]).reshape(b_s, mem_len, d)
        v_s = bf(cache_mem_v[i]).reshape(b_s, mem_len, d)
        half = tm // 2
        for tile0, n_tiles, kk, vv, segments in (
                (0, n_ptiles, k_p, v_p, ((0, half, 0), (half, half, 0))),
                (n_ptiles, 1, k_s, v_s, tuple((b * dec_seq, dec_seq, b) for b in range(b_s)))):
            (x,) = _prologue_matmul(
                x, xa_w_o, (i,), [q, kk, vv],
                [_tile_spec(tm, d, tile0), _full_spec(kk.shape), _full_spec(vv.shape)],
                functools.partial(_attn_prologue, segments=segments, n_heads=n_heads),
                name="attn_out_shared_kv" if tile0 == 0 else "attn_out_batched_kv",
                tile0=tile0, n_tiles=n_tiles, tm=tm, tn=tn_d)

        j = i // 2
        if i % 2 == 0:
            tn_f = _pick(d_ff, (512, 256, 128))
            t = _norm_matmul(x, norm_ffn[i], [(ffn_w1, (j,), 0), (ffn_w3, (j,), 0)], d_ff, _swiglu,
                             BF16, tm, tn_f)
            (x,) = _prologue_matmul(
                x, ffn_w2, (j,), [t], [_tile_spec(tm, d_ff, 0)], None,
                name="ffn_down", tile0=0, n_tiles=t_all // tm, tm=tm, tn=tn_d)
        else:
            x = _moe_ffn(x, norm_ffn[i], moe_router[j], moe_w1, moe_w3, moe_w2, j, norm_out, tm,
                         final_norm=(i == depth - 1))

    y_prompt = x[:t_p].reshape(1, seq, d)
    y_sample = x[t_p:].reshape(b_s, dec_seq, d)
    return (y_prompt, y_sample, new_mem_k, new_mem_v,
            jnp.stack(pool_tails_p), jnp.stack(pool_tails_s),
            jnp.stack(conv_p), jnp.stack(conv_s), jnp.stack(gm_s))
```

```python
import functools
import math

import jax
import jax.numpy as jnp
from jax import lax
from jax.experimental import pallas as pl
from jax.experimental.pallas import tpu as pltpu

F32 = jnp.float32
BF16 = jnp.bfloat16

EPS = 1e-6
PAST_LEN = 2048
CHUNK = 64
GM_BLOCK = 128
GM_GROUPS = 4
POOL_WINDOWS = (2, 4, 8, 16)
TOP_K = 2

TILE_M = 512
HALO = 32
PAD = 8
ROW_CHUNK = 16
SUBLANES = 8
LANES = 128
VMEM_LIMIT = 60 * 1024 * 1024
NORM_MATMUL_VMEM_BUDGET = 48 * 1024 * 1024


def _params(n_grid):
    return pltpu.CompilerParams(
        dimension_semantics=("arbitrary",) * n_grid, vmem_limit_bytes=VMEM_LIMIT)


def _pick(n, prefs):
    for p in prefs:
        if n % p == 0:
            return p
    return n


def _stacked_spec(w, lead, block, index_fn):
    return pl.BlockSpec((None,) * len(lead) + tuple(block),
                        lambda *a: tuple(lead) + tuple(index_fn(*a)))


def _rms_rows(x_ref, g_ref, out_ref, inv_ref, n_rows, x_row0=0, out_row0=0):
    d = x_ref.shape[1]
    n_cc = d // LANES

    def stats(c, carry):
        r = pl.multiple_of(c * ROW_CHUNK, ROW_CHUNK)
        part = jnp.zeros((ROW_CHUNK, LANES), F32)
        for cc in range(n_cc):
            x = x_ref[pl.ds(x_row0 + r, ROW_CHUNK), cc * LANES:(cc + 1) * LANES].astype(F32)
            part = part + x * x
        ms = jnp.sum(part, axis=-1, keepdims=True) / d
        inv_ref[pl.ds(r, ROW_CHUNK), :] = jnp.broadcast_to(lax.rsqrt(ms + EPS), (ROW_CHUNK, LANES))
        return carry
    lax.fori_loop(0, n_rows // ROW_CHUNK, stats, 0, unroll=4)

    def scale(c, carry):
        r = pl.multiple_of(c * ROW_CHUNK, ROW_CHUNK)
        inv = inv_ref[pl.ds(r, ROW_CHUNK), :]
        for cc in range(n_cc):
            cols = slice(cc * LANES, (cc + 1) * LANES)
            x = x_ref[pl.ds(x_row0 + r, ROW_CHUNK), cols].astype(F32)
            out_ref[pl.ds(out_row0 + r, ROW_CHUNK), cols] = (
                (x * inv) * g_ref[:, cols]).astype(out_ref.dtype)
        return carry
    lax.fori_loop(0, n_rows // ROW_CHUNK, scale, 0, unroll=2)


def _cast_rows(x_ref, out_ref, n_rows):
    def body(c, carry):
        r = pl.multiple_of(c * ROW_CHUNK, ROW_CHUNK)
        out_ref[pl.ds(r, ROW_CHUNK), :] = x_ref[pl.ds(r, ROW_CHUNK), :].astype(out_ref.dtype)
        return carry
    lax.fori_loop(0, n_rows // ROW_CHUNK, body, 0, unroll=4)


def _sigmoid(x):
    return 1.0 / (1.0 + jnp.exp(-x))


def _silu(x):
    return x * _sigmoid(x)


def _gelu_tanh(x):
    c = math.sqrt(2.0 / math.pi)
    return 0.5 * x * (1.0 + jnp.tanh(c * (x + 0.044715 * (x * x * x))))


def _glu(a, gate):
    return a * _sigmoid(gate)


def _swiglu(a, b):
    return _silu(a) * b


def _identity(a):
    return a


def _norm_matmul_kernel(*refs, n_w, epilogue, tm):
    x_ref, g_ref = refs[0], refs[1]
    w_refs = refs[2:2 + n_w]
    o_ref = refs[2 + n_w]
    h_ref, inv_ref = refs[3 + n_w], refs[4 + n_w]

    @pl.when(pl.program_id(1) == 0)
    def _():
        _rms_rows(x_ref, g_ref, h_ref, inv_ref, tm)

    h = h_ref[...]
    accs = [jnp.dot(h, w[...], preferred_element_type=F32) for w in w_refs]
    o_ref[...] = epilogue(*accs).astype(o_ref.dtype)


def _norm_matmul_tiles(m, k, n_out, n_w, out_itemsize, col0s):
    for tm in (1536, 1024, 768, 512, 256, 128):
        for tn in (1024, 512, 256, 128):
            if m % tm or n_out % tn or any(c % tn for c in col0s):
                continue
            need = (2 * tm * k * 4 + tm * k * 2 + tm * LANES * 4
                    + n_w * 2 * k * tn * 2 + 2 * tm * tn * out_itemsize)
            if need <= NORM_MATMUL_VMEM_BUDGET:
                return tm, tn
    raise ValueError("no tile of the norm-matmul fits VMEM")


def _norm_matmul(x, g, ws, n_out, epilogue, out_dtype):
    m, k = x.shape
    tm, tn = _norm_matmul_tiles(m, k, n_out, len(ws), jnp.dtype(out_dtype).itemsize,
                                [c for _, _, c in ws])
    grid = (m // tm, n_out // tn)
    w_specs = [_stacked_spec(w, lead, (k, tn),
                             functools.partial(lambda i, j, cb: (0, j + cb), cb=c0 // tn))
               for w, lead, c0 in ws]
    return pl.pallas_call(
        functools.partial(_norm_matmul_kernel, n_w=len(ws), epilogue=epilogue, tm=tm),
        grid=grid,
        in_specs=[pl.BlockSpec((tm, k), lambda i, j: (i, 0)),
                  pl.BlockSpec((1, k), lambda i, j: (0, 0))] + w_specs,
        out_specs=pl.BlockSpec((tm, tn), lambda i, j: (i, j)),
        out_shape=jax.ShapeDtypeStruct((m, n_out), out_dtype),
        scratch_shapes=[pltpu.VMEM((tm, k), BF16), pltpu.VMEM((tm, LANES), F32)],
        compiler_params=_params(2),
        name="norm_matmul" + epilogue.__name__,
    )(x, g.reshape(1, k), *[w for w, _, _ in ws])


def _prologue_matmul_kernel(*refs, n_pro, n_extra, prologue):
    pro_refs = refs[:n_pro]
    w_ref, res_ref, o_ref = refs[n_pro], refs[n_pro + 1], refs[n_pro + 2]
    extra_refs = refs[n_pro + 3:n_pro + 3 + n_extra]
    scratch = refs[n_pro + 3 + n_extra:]
    if prologue is None:
        a_ref = pro_refs[0]
    else:
        a_ref = scratch[0]

        @pl.when(pl.program_id(1) == 0)
        def _():
            prologue(pro_refs, extra_refs, a_ref, scratch[1:])

    o_ref[...] = res_ref[...] + jnp.dot(a_ref[...], w_ref[...], preferred_element_type=F32)


def _prologue_matmul(x_res, w, w_lead, pro_inputs, pro_specs, prologue, *, name, tile0, n_tiles,
                     tm, tn, extra_shapes=(), extra_specs=(), scratch_shapes=()):
    k, n = w.shape[-2:]
    n_pro = len(pro_inputs)
    a_scratch = [] if prologue is None else [pltpu.VMEM((tm, k), BF16)]
    return pl.pallas_call(
        functools.partial(_prologue_matmul_kernel, n_pro=n_pro, n_extra=len(extra_shapes),
                          prologue=prologue),
        grid=(n_tiles, n // tn),
        in_specs=list(pro_specs) + [
            _stacked_spec(w, w_lead, (k, tn), lambda i, j: (0, j)),
            pl.BlockSpec((tm, tn), lambda i, j: (i + tile0, j))],
        out_specs=[pl.BlockSpec((tm, tn), lambda i, j: (i + tile0, j))] + list(extra_specs),
        out_shape=[jax.ShapeDtypeStruct(x_res.shape, x_res.dtype)] + list(extra_shapes),
        scratch_shapes=a_scratch + list(scratch_shapes),
        input_output_aliases={n_pro + 1: 0},
        compiler_params=_params(2),
        name=name,
    )(*pro_inputs, w, x_res)


def _tile_spec(tm, width, tile0):
    return pl.BlockSpec((tm, width), lambda i, j: (i + tile0, 0))


def _full_spec(shape):
    return pl.BlockSpec(tuple(shape), lambda i, j: (0,) * len(shape))


def _pool_kernel(x_ref, g_ref, st_ref, w_ref, sc_ref, o_ref, tail_ref, hp_ref, sa_ref, sb_ref,
                 d_ref, inv_ref, *, tm, seg_len, carry, start_pos):
    d_model = x_ref.shape[1]
    n_groups = len(POOL_WINDOWS)
    gw = d_model // n_groups
    base = PAD + HALO
    n_seg = tm // seg_len
    i = pl.program_id(0)

    hp_ref[0:PAD, :] = jnp.zeros((PAD, d_model), F32)
    sa_ref[0:PAD, :] = jnp.zeros((PAD, gw), F32)
    sb_ref[0:PAD, :] = jnp.zeros((PAD, gw), F32)

    for s in range(n_seg):
        row0 = s * seg_len
        if carry:
            @pl.when(i == 0)
            def _():
                hp_ref[PAD:base, :] = jnp.zeros((HALO, d_model), F32)

            @pl.when(i > 0)
            def _():
                hp_ref[PAD:base, :] = hp_ref[PAD + seg_len:base + seg_len, :]
        else:
            hp_ref[PAD:base, :] = st_ref[s]

        _rms_rows(x_ref, g_ref, hp_ref, inv_ref, seg_len, x_row0=row0, out_row0=base)
        tail_ref[s] = hp_ref[base + seg_len - 16:base + seg_len, :]

        if carry:
            pos0 = start_pos + i * tm + row0
        else:
            pos0 = start_pos
        pos = (pos0 + lax.broadcasted_iota(jnp.int32, (seg_len, 1), 0)).astype(F32)

        n_steps = (HALO + seg_len) // 32
        for g, win in enumerate(POOL_WINDOWS):
            c0 = g * gw
            bufs = (sa_ref, sb_ref)
            n_stage = int(math.log2(win))
            for t in range(n_stage):
                shift = 1 << t
                dst = bufs[t % 2]
                for c in range(n_steps):
                    r = PAD + c * 32
                    if t == 0:
                        a = hp_ref[r:r + 32, c0:c0 + gw]
                        b = hp_ref[r - shift:r - shift + 32, c0:c0 + gw]
                    else:
                        src = bufs[(t - 1) % 2]
                        a = src[r:r + 32, :]
                        b = src[r - shift:r - shift + 32, :]
                    dst[r:r + 32, :] = a + b
            sums = bufs[(n_stage - 1) % 2]
            cnt = jnp.minimum(pos + 1.0, float(win))
            mean = sums[base:base + seg_len, :] / cnt
            d_ref[row0:row0 + seg_len, c0:c0 + gw] = (
                mean - hp_ref[base:base + seg_len, c0:c0 + gw]).astype(BF16)

    for g in range(n_groups):
        c0 = g * gw
        mix = jnp.dot(d_ref[:, c0:c0 + gw], w_ref[g], preferred_element_type=F32)
        o_ref[:, c0:c0 + gw] = x_ref[:, c0:c0 + gw] + mix * sc_ref[:, c0:c0 + gw]


def _pool_mix(x_all, g, state, w_bf, slot, scale, *, tile0, n_tiles, tm, seg_len, carry, start_pos):
    t_all, d = x_all.shape
    n_seg = tm // seg_len
    gw = d // len(POOL_WINDOWS)
    return pl.pallas_call(
        functools.partial(_pool_kernel, tm=tm, seg_len=seg_len, carry=carry, start_pos=start_pos),
        grid=(n_tiles,),
        in_specs=[pl.BlockSpec((tm, d), lambda i: (i + tile0, 0)),
                  pl.BlockSpec((1, d), lambda i: (0, 0)),
                  pl.BlockSpec(state.shape, lambda i: (0, 0, 0)),
                  _stacked_spec(w_bf, (slot,), w_bf.shape[1:], lambda i: (0, 0, 0)),
                  pl.BlockSpec((1, d), lambda i: (0, 0))],
        out_specs=[pl.BlockSpec((tm, d), lambda i: (i + tile0, 0)),
                   pl.BlockSpec((n_seg, 16, d), lambda i: (0, 0, 0))],
        out_shape=[jax.ShapeDtypeStruct((t_all, d), F32),
                   jax.ShapeDtypeStruct((n_seg, 16, d), F32)],
        scratch_shapes=[pltpu.VMEM((PAD + HALO + seg_len, d), F32),
                        pltpu.VMEM((PAD + HALO + seg_len, gw), F32),
                        pltpu.VMEM((PAD + HALO + seg_len, gw), F32),
                        pltpu.VMEM((tm, d), BF16),
                        pltpu.VMEM((seg_len, LANES), F32)],
        input_output_aliases={0: 0},
        compiler_params=_params(1),
        name="pool_mix_carry" if carry else "pool_mix_state",
    )(x_all, g.reshape(1, d), state, w_bf, scale.reshape(1, d))


CONV_STRIDE = 4


def _conv_prologue(pro_refs, extra_refs, a_ref, scratch, *, tm, seg_len, carry, width):
    u_ref, st_ref, wb_ref, bb_ref, gn_ref = pro_refs
    up_ref, c_ref = scratch
    d_model = u_ref.shape[1]
    n_slabs = d_model // LANES
    n_seg = tm // seg_len
    first = HALO - (width - 1)
    rows_per_step = SUBLANES * CONV_STRIDE
    i = pl.program_id(0)

    for s in range(n_seg):
        row0 = s * seg_len
        if carry:
            @pl.when(i == 0)
            def _():
                up_ref[:, 0:HALO, :] = jnp.zeros((n_slabs, HALO, LANES), F32)

            @pl.when(i > 0)
            def _():
                up_ref[:, 0:HALO, :] = up_ref[:, seg_len:seg_len + HALO, :]
        else:
            up_ref[:, 0:HALO, :] = st_ref[s]
        for sl in range(n_slabs):
            up_ref[sl, HALO:HALO + seg_len, :] = u_ref[row0:row0 + seg_len, sl * LANES:(sl + 1) * LANES]

        def slab_body(sl, cr):
            def step_body(c, cr2):
                r0 = pl.multiple_of(c * rows_per_step, rows_per_step)
                accs = [[bb_ref[sl], jnp.zeros((SUBLANES, LANES), F32)] for _ in range(CONV_STRIDE)]
                for m in range(width + CONV_STRIDE - 1):
                    x = up_ref[sl, pl.ds(r0 + first + m, SUBLANES, stride=CONV_STRIDE), :]
                    for ph in range(CONV_STRIDE):
                        k = m - ph
                        if 0 <= k < width:
                            accs[ph][k % 2] = accs[ph][k % 2] + (
                                x * wb_ref[sl, SUBLANES * k:SUBLANES * (k + 1), :])
                for ph in range(CONV_STRIDE):
                    c_ref[sl, pl.ds(r0 + ph, SUBLANES, stride=CONV_STRIDE), :] = accs[ph][0] + accs[ph][1]
                return cr2
            lax.fori_loop(0, seg_len // rows_per_step, step_body, 0, unroll=2)
            return cr
        lax.fori_loop(0, n_slabs, slab_body, 0)

        def norm_body(c, cr):
            r = pl.multiple_of(c * ROW_CHUNK, ROW_CHUNK)
            part = jnp.zeros((ROW_CHUNK, LANES), F32)
            for sl in range(n_slabs):
                v = c_ref[sl, pl.ds(r, ROW_CHUNK), :]
                part = part + v * v
            inv = lax.rsqrt(jnp.sum(part, axis=-1, keepdims=True) / d_model + EPS)
            for sl in range(n_slabs):
                cols = slice(sl * LANES, (sl + 1) * LANES)
                y = (c_ref[sl, pl.ds(r, ROW_CHUNK), :] * inv) * gn_ref[:, cols]
                a_ref[pl.ds(row0 + r, ROW_CHUNK), cols] = _silu(y).astype(BF16)
            return cr
        lax.fori_loop(0, seg_len // ROW_CHUNK, norm_body, 0, unroll=2)


def _to_slabs(a):
    *lead, rows, d = a.shape
    a = a.reshape(*lead, rows, d // LANES, LANES)
    return jnp.swapaxes(a, -3, -2)


GM_COLS = 512


def _gmlp_prologue(pro_refs, extra_refs, a_ref, scratch, *, tm, emit_v):
    z_ref, gv_ref, ws_ref, bs_ref = pro_refs
    (vn_ref,) = scratch
    d_gm = gv_ref.shape[1]
    gcols = d_gm // GM_GROUPS
    n_cc = d_gm // 1024

    def norm_body(c, cr):
        r = pl.multiple_of(c * ROW_CHUNK, ROW_CHUNK)
        ssq = jnp.zeros((ROW_CHUNK, 1), F32)
        for cc in range(n_cc):
            v = z_ref[pl.ds(r, ROW_CHUNK), d_gm + cc * 1024:d_gm + (cc + 1) * 1024].astype(F32)
            ssq = ssq + jnp.sum(v * v, axis=-1, keepdims=True)
        inv = lax.rsqrt(ssq / d_gm + EPS)
        for cc in range(n_cc):
            cols = slice(cc * 1024, (cc + 1) * 1024)
            v = z_ref[pl.ds(r, ROW_CHUNK), d_gm + cc * 1024:d_gm + (cc + 1) * 1024].astype(F32)
            vn = (v * inv) * gv_ref[:, cols]
            vn_ref[pl.ds(r, ROW_CHUNK), cols] = vn.astype(BF16)
            if emit_v:
                extra_refs[0][pl.ds(r, ROW_CHUNK), cols] = vn
        return cr
    lax.fori_loop(0, tm // ROW_CHUNK, norm_body, 0, unroll=2)

    for g in range(GM_GROUPS):
        for cc in range(gcols // GM_COLS):
            cols = slice(g * gcols + cc * GM_COLS, g * gcols + (cc + 1) * GM_COLS)
            s = jnp.dot(ws_ref[g], vn_ref[:, cols], preferred_element_type=F32)
            s = s + bs_ref[:, g:g + 1]
            a_ref[:, cols] = (z_ref[:, cols].astype(F32) * s).astype(BF16)


def _gmlp_spatial_weights(w_s, b_s, blk, tm):
    cidx = jnp.arange(blk) // CHUNK
    mask = cidx[None, :] <= cidx[:, None]
    ws = jnp.where(mask[None], w_s[:, :blk, :blk], 0.0)
    eye = jnp.eye(tm // blk, dtype=ws.dtype)
    big = jnp.einsum("ab,gij->gaibj", eye, ws).reshape(w_s.shape[0], tm, tm)
    bias = jnp.tile(b_s[:, :blk].T, (tm // blk, 1))
    return big.astype(BF16), bias.astype(F32)


def _attn_prologue(pro_refs, extra_refs, a_ref, scratch, *, segments, n_heads):
    q_ref, k_ref, v_ref = pro_refs
    d_model = q_ref.shape[1]
    hd = d_model // n_heads
    scale = hd ** -0.5
    for row0, n_rows, kv in segments:
        for h in range(n_heads):
            cols = slice(h * hd, (h + 1) * hd)
            q = q_ref[row0:row0 + n_rows, cols]
            s = lax.dot_general(q, k_ref[kv, :, cols], (((1,), (1,)), ((), ())),
                                preferred_element_type=F32) * scale
            m = jnp.max(s, axis=-1, keepdims=True)
            e = jnp.exp(s - m)
            p = e / jnp.sum(e, axis=-1, keepdims=True)
            o = jnp.dot(p.astype(BF16), v_ref[kv, :, cols], preferred_element_type=F32)
            a_ref[row0:row0 + n_rows, cols] = o.astype(BF16)


def _router_kernel(x_ref, g_ref, wr_ref, h_ref, meta_ref, gate_ref, cnt_ref, run_ref, inv_ref, *,
                   tm, n_experts):
    i = pl.program_id(0)

    @pl.when(i == 0)
    def _():
        run_ref[...] = jnp.zeros(run_ref.shape, F32)

    _rms_rows(x_ref, g_ref, h_ref, inv_ref, tm)
    logits = jnp.dot(h_ref[...], wr_ref[...], preferred_element_type=F32,
                     precision=lax.Precision.HIGHEST)
    lane = lax.broadcasted_iota(jnp.int32, (tm, LANES), 1)
    neg = jnp.float32(-jnp.inf)
    logits = jnp.where(lane < n_experts, logits, neg)
    m1 = jnp.max(logits, axis=-1, keepdims=True)
    i1 = jnp.min(jnp.where(logits == m1, lane, LANES), axis=-1, keepdims=True)
    rest = jnp.where(lane == i1, neg, logits)
    m2 = jnp.max(rest, axis=-1, keepdims=True)
    i2 = jnp.min(jnp.where(rest == m2, lane, LANES), axis=-1, keepdims=True)
    e2 = jnp.exp(m2 - m1)
    den = 1.0 + e2
    g1 = 1.0 / den
    g2 = e2 / den

    sel1 = lane == i1
    sel2 = lane == i2
    onehot = jnp.where(sel1 | sel2, 1.0, 0.0)
    r_i = lax.broadcasted_iota(jnp.int32, (tm, tm), 0)
    c_i = lax.broadcasted_iota(jnp.int32, (tm, tm), 1)
    tri = jnp.where(c_i < r_i, 1.0, 0.0).astype(BF16)
    before = jnp.dot(tri, onehot.astype(BF16), preferred_element_type=F32) + run_ref[0:1, :]
    rank1 = jnp.sum(jnp.where(sel1, before, 0.0), axis=-1, keepdims=True)
    rank2 = jnp.sum(jnp.where(sel2, before, 0.0), axis=-1, keepdims=True)
    run_ref[0:1, :] = run_ref[0:1, :] + jnp.sum(onehot, axis=0, keepdims=True)
    cnt_ref[...] = jnp.broadcast_to(run_ref[0:1, :], cnt_ref.shape)

    col = lax.broadcasted_iota(jnp.int32, (tm, 8), 1)
    meta_ref[...] = jnp.where(col == 0, i1.astype(F32),
                    jnp.where(col == 1, i2.astype(F32),
                    jnp.where(col == 2, rank1,
                    jnp.where(col == 3, rank2, 0.0))))
    gate_ref[:, 0:LANES] = jnp.broadcast_to(g1, (tm, LANES))
    gate_ref[:, LANES:2 * LANES] = jnp.broadcast_to(g2, (tm, LANES))


def _router(x_all, g, w_r, tm):
    t_all, d = x_all.shape
    n_experts = w_r.shape[1]
    wr_pad = jnp.zeros((d, LANES), F32).at[:, :n_experts].set(w_r)
    return pl.pallas_call(
        functools.partial(_router_kernel, tm=tm, n_experts=n_experts),
        grid=(t_all // tm,),
        in_specs=[pl.BlockSpec((tm, d), lambda i: (i, 0)),
                  pl.BlockSpec((1, d), lambda i: (0, 0)),
                  pl.BlockSpec((d, LANES), lambda i: (0, 0))],
        out_specs=[pl.BlockSpec((tm, d), lambda i: (i, 0)),
                   pl.BlockSpec((tm, 8), lambda i: (i, 0)),
                   pl.BlockSpec((tm, 2 * LANES), lambda i: (i, 0)),
                   pl.BlockSpec((8, LANES), lambda i: (0, 0))],
        out_shape=[jax.ShapeDtypeStruct((t_all, d), F32),
                   jax.ShapeDtypeStruct((t_all, 8), F32),
                   jax.ShapeDtypeStruct((t_all, 2 * LANES), F32),
                   jax.ShapeDtypeStruct((8, LANES), F32)],
        scratch_shapes=[pltpu.VMEM((8, LANES), F32), pltpu.VMEM((tm, LANES), F32)],
        compiler_params=_params(1),
        name="moe_router",
    )(x_all, g.reshape(1, d), wr_pad)


def _dispatch_kernel(p1_ref, p2_ref, ps_ref, pc_ref, nu_ref, h_ref, buf_ref, zero_ref, sem, *, tm,
                     n_experts, n_tiles):
    t0 = pl.program_id(0) * tm

    @pl.when(pl.program_id(0) == 0)
    def _():
        zero_ref[...] = jnp.zeros(zero_ref.shape, zero_ref.dtype)
        for e in range(n_experts):
            def pad_copy(r, e=e):
                return pltpu.make_async_copy(zero_ref.at[pl.ds(0, 1)],
                                             buf_ref.at[pl.ds(ps_ref[e] + r, 1)], sem)

            def pad_start(r, cr):
                pad_copy(r).start()
                return cr
            lax.fori_loop(0, pc_ref[e], pad_start, 0)

            def pad_wait(r, cr):
                pad_copy(r).wait()
                return cr
            lax.fori_loop(0, pc_ref[e], pad_wait, 0)

        def tile_copy(k):
            return pltpu.make_async_copy(zero_ref, buf_ref.at[pl.ds(k * tm, tm)], sem)

        def tile_start(k, cr):
            tile_copy(k).start()
            return cr
        lax.fori_loop(nu_ref[0], n_tiles, tile_start, 0)

        def tile_wait(k, cr):
            tile_copy(k).wait()
            return cr
        lax.fori_loop(nu_ref[0], n_tiles, tile_wait, 0)

    def copies(r):
        t = t0 + r
        return (pltpu.make_async_copy(h_ref.at[pl.ds(r, 1)], buf_ref.at[pl.ds(p1_ref[t], 1)], sem),
                pltpu.make_async_copy(h_ref.at[pl.ds(r, 1)], buf_ref.at[pl.ds(p2_ref[t], 1)], sem))

    def start(r, cr):
        for c in copies(r):
            c.start()
        return cr
    lax.fori_loop(0, tm, start, 0, unroll=8)

    def wait(r, cr):
        for c in copies(r):
            c.wait()
        return cr
    lax.fori_loop(0, tm, wait, 0, unroll=8)


def _dispatch(h, p1, p2, pad_start, pad_count, n_used, n_slots, tm):
    t_all, d = h.shape
    return pl.pallas_call(
        functools.partial(_dispatch_kernel, tm=tm, n_experts=pad_start.shape[0],
                          n_tiles=n_slots // tm),
        grid_spec=pltpu.PrefetchScalarGridSpec(
            num_scalar_prefetch=5,
            grid=(t_all // tm,),
            in_specs=[pl.BlockSpec((tm, d), lambda i, *_: (i, 0))],
            out_specs=pl.BlockSpec(memory_space=pl.ANY),
            scratch_shapes=[pltpu.VMEM((tm, d), h.dtype), pltpu.SemaphoreType.DMA(())]),
        out_shape=jax.ShapeDtypeStruct((n_slots, d), h.dtype),
        compiler_params=_params(1),
        name="moe_dispatch",
    )(p1, p2, pad_start, pad_count, n_used, h)


def _expert_up_kernel(te_ref, nu_ref, x_ref, w1_ref, w3_ref, o_ref, h_ref, *, tm):
    i = pl.program_id(0)

    @pl.when(i < nu_ref[0])
    def _():
        @pl.when(pl.program_id(1) == 0)
        def _():
            _cast_rows(x_ref, h_ref, tm)
        h = h_ref[...]
        a = jnp.dot(h, w1_ref[...], preferred_element_type=F32)
        b = jnp.dot(h, w3_ref[...], preferred_element_type=F32)
        o_ref[...] = _swiglu(a, b).astype(o_ref.dtype)

    @pl.when(i >= nu_ref[0])
    def _():
        o_ref[...] = jnp.zeros(o_ref.shape, o_ref.dtype)


def _expert_up(buf, w1, w3, layer, tile_expert, n_used, tm, tn):
    n_slots, d = buf.shape
    f = w1.shape[-1]
    w_spec = lambda: pl.BlockSpec((None, None, d, tn), lambda i, j, te, nu: (layer, te[i], 0, j))
    return pl.pallas_call(
        functools.partial(_expert_up_kernel, tm=tm),
        grid_spec=pltpu.PrefetchScalarGridSpec(
            num_scalar_prefetch=2,
            grid=(n_slots // tm, f // tn),
            in_specs=[pl.BlockSpec((tm, d), lambda i, j, te, nu: (i, 0)), w_spec(), w_spec()],
            out_specs=pl.BlockSpec((tm, tn), lambda i, j, te, nu: (i, j)),
            scratch_shapes=[pltpu.VMEM((tm, d), BF16)]),
        out_shape=jax.ShapeDtypeStruct((n_slots, f), BF16),
        compiler_params=_params(2),
        name="moe_expert_up",
    )(tile_expert, n_used, buf, w1, w3)


def _expert_down_kernel(te_ref, nu_ref, x_ref, w_ref, o_ref):
    i = pl.program_id(0)

    @pl.when(i < nu_ref[0])
    def _():
        o_ref[...] = jnp.dot(x_ref[...], w_ref[...], preferred_element_type=F32)

    @pl.when(i >= nu_ref[0])
    def _():
        o_ref[...] = jnp.zeros(o_ref.shape, o_ref.dtype)


def _expert_down(t_sorted, w2, layer, tile_expert, n_used, tm, tn):
    n_slots, f = t_sorted.shape
    d = w2.shape[-1]
    return pl.pallas_call(
        _expert_down_kernel,
        grid_spec=pltpu.PrefetchScalarGridSpec(
            num_scalar_prefetch=2,
            grid=(n_slots // tm, d // tn),
            in_specs=[pl.BlockSpec((tm, f), lambda i, j, te, nu: (i, 0)),
                      pl.BlockSpec((None, None, f, tn), lambda i, j, te, nu: (layer, te[i], 0, j))],
            out_specs=pl.BlockSpec((tm, tn), lambda i, j, te, nu: (i, j))),
        out_shape=jax.ShapeDtypeStruct((n_slots, d), F32),
        compiler_params=_params(2),
        name="moe_expert_down",
    )(tile_expert, n_used, t_sorted, w2)


def _combine_kernel(p1_ref, p2_ref, x_ref, gate_ref, y_ref, gout_ref, o_ref, a_ref, b_ref, sems, *,
                    tm, n_tiles, final_norm):
    i = pl.program_id(0)
    slot = i % 2

    def copies(tile, sl, r):
        t = tile * tm + r
        return (pltpu.make_async_copy(y_ref.at[pl.ds(p1_ref[t], 1)], a_ref.at[sl, pl.ds(r, 1)],
                                      sems.at[sl]),
                pltpu.make_async_copy(y_ref.at[pl.ds(p2_ref[t], 1)], b_ref.at[sl, pl.ds(r, 1)],
                                      sems.at[sl]))

    def start_gather(tile, sl):
        def start(r, cr):
            for c in copies(tile, sl, r):
                c.start()
            return cr
        lax.fori_loop(0, tm, start, 0, unroll=8)

    @pl.when(i == 0)
    def _():
        start_gather(0, 0)

    @pl.when(i + 1 < n_tiles)
    def _():
        start_gather(i + 1, 1 - slot)

    def wait(r, cr):
        for c in copies(i, slot, r):
            c.wait()
        return cr
    lax.fori_loop(0, tm, wait, 0, unroll=8)

    d = x_ref.shape[1]
    n_cc = d // LANES

    def body(c, cr):
        r = pl.multiple_of(c * ROW_CHUNK, ROW_CHUNK)
        rows = pl.ds(r, ROW_CHUNK)
        g1 = gate_ref[rows, 0:LANES]
        g2 = gate_ref[rows, LANES:2 * LANES]
        part = jnp.zeros((ROW_CHUNK, LANES), F32)
        for cc in range(n_cc):
            cols = slice(cc * LANES, (cc + 1) * LANES)
            x = x_ref[rows, cols] + (g1 * a_ref[slot, rows, cols] + g2 * b_ref[slot, rows, cols])
            o_ref[rows, cols] = x
            if final_norm:
                part = part + x * x
        if final_norm:
            ms = jnp.sum(part, axis=-1, keepdims=True) / d
            inv = jnp.broadcast_to(lax.rsqrt(ms + EPS), (ROW_CHUNK, LANES))
            for cc in range(n_cc):
                cols = slice(cc * LANES, (cc + 1) * LANES)
                o_ref[rows, cols] = (o_ref[rows, cols] * inv) * gout_ref[:, cols]
        return cr
    lax.fori_loop(0, tm // ROW_CHUNK, body, 0, unroll=2)


def _combine(x_all, gates, y_sorted, p1, p2, g_out, tm, final_norm):
    t_all, d = x_all.shape
    return pl.pallas_call(
        functools.partial(_combine_kernel, tm=tm, n_tiles=t_all // tm, final_norm=final_norm),
        grid_spec=pltpu.PrefetchScalarGridSpec(
            num_scalar_prefetch=2,
            grid=(t_all // tm,),
            in_specs=[pl.BlockSpec((tm, d), lambda i, a, b: (i, 0)),
                      pl.BlockSpec((tm, 2 * LANES), lambda i, a, b: (i, 0)),
                      pl.BlockSpec(memory_space=pl.ANY),
                      pl.BlockSpec((1, d), lambda i, a, b: (0, 0))],
            out_specs=pl.BlockSpec((tm, d), lambda i, a, b: (i, 0)),
            scratch_shapes=[pltpu.VMEM((2, tm, d), F32), pltpu.VMEM((2, tm, d), F32),
                            pltpu.SemaphoreType.DMA((2,))]),
        out_shape=jax.ShapeDtypeStruct((t_all, d), F32),
        input_output_aliases={2: 0},
        compiler_params=_params(1),
        name="moe_combine",
    )(p1, p2, x_all, gates, y_sorted, g_out.reshape(1, d))


def _moe_ffn(x_all, g, w_r, w1, w3, w2, layer, g_out, tm, final_norm):
    t_all, d = x_all.shape
    n_experts = w_r.shape[1]
    h, meta, gates, counts = _router(x_all, g, w_r, tm)

    cnt = counts[0, :n_experts].astype(jnp.int32)
    padded = ((cnt + tm - 1) // tm) * tm
    ends = jnp.cumsum(padded)
    offs = ends - padded
    e1 = meta[:, 0].astype(jnp.int32)
    e2 = meta[:, 1].astype(jnp.int32)
    p1 = offs[e1] + meta[:, 2].astype(jnp.int32)
    p2 = offs[e2] + meta[:, 3].astype(jnp.int32)
    n_tiles = (TOP_K * t_all) // tm + n_experts
    tile_start = jnp.arange(n_tiles, dtype=jnp.int32) * tm
    tile_expert = jnp.minimum(
        jnp.sum((tile_start[:, None] >= ends[None, :]).astype(jnp.int32), axis=1), n_experts - 1)
    n_used = (ends[-1] // tm).reshape(1).astype(jnp.int32)

    buf = _dispatch(h, p1, p2, offs + cnt, padded - cnt, n_used, n_tiles * tm, tm)
    f = w1.shape[-1]
    t_sorted = _expert_up(buf, w1, w3, layer, tile_expert, n_used, tm,
                          _pick(f, (1408, 512, 256, 128)))
    y_sorted = _expert_down(t_sorted, w2, layer, tile_expert, n_used, tm,
                            _pick(d, (1024, 512, 256, 128)))
    return _combine(x_all, gates, y_sorted, p1, p2, g_out, tm, final_norm)


def kernel(x_prompt, x_sample, cache_mem_k, cache_mem_v, state_pool, state_conv, mem_prompt,
           norm_mix, norm_xattn, norm_mem, norm_ffn, norm_out, pool_w, pool_scale,
           conv_w_in, conv_w_dw, conv_b_dw, conv_norm, conv_w_out,
           gm_w_in, gm_norm, gm_w_s, gm_b_s, gm_w_out, xa_w_q, xa_w_kv, xa_w_o,
           ffn_w1, ffn_w3, ffn_w2, moe_router, moe_w1, moe_w3, moe_w2):
    b_p, seq, d = x_prompt.shape
    b_s, dec_seq, _ = x_sample.shape
    depth = norm_mix.shape[0]
    mem_len = mem_prompt.shape[1]
    n_heads = cache_mem_k.shape[3]
    pool_state = state_pool.shape[2]
    conv_width = conv_w_dw.shape[1]
    conv_state = state_conv.shape[2]
    d_gm = gm_norm.shape[1]
    d_ff = ffn_w1.shape[2]
    tm = TILE_M
    assert b_p == 1 and seq % tm == 0 and b_s * dec_seq == tm
    assert conv_state <= HALO and pool_state <= 16 and seq >= conv_state
    assert dec_seq % (SUBLANES * CONV_STRIDE) == 0 and d % LANES == 0
    assert depth % 2 == 0, "the output norm is fused into the last layer's expert FFN"
    n_ptiles = seq // tm
    t_p = seq
    t_all = seq + tm

    x = jnp.concatenate([x_prompt.reshape(seq, d), x_sample.reshape(tm, d)], axis=0)
    bf = lambda a: a.astype(BF16)
    pool_w, conv_w_in, conv_w_out, gm_w_in, gm_w_out = map(
        bf, (pool_w, conv_w_in, conv_w_out, gm_w_in, gm_w_out))
    xa_w_q, xa_w_kv, xa_w_o, ffn_w1, ffn_w3, ffn_w2, moe_w1, moe_w3, moe_w2 = map(
        bf, (xa_w_q, xa_w_kv, xa_w_o, ffn_w1, ffn_w3, ffn_w2, moe_w1, moe_w3, moe_w2))

    tn_d = _pick(d, (1024, 512, 256, 128))

    mem = mem_prompt.reshape(mem_len, d)
    kv_p = [_norm_matmul(mem, norm_mem[i], [(xa_w_kv, (i,), 0)], 2 * d, _identity, F32)
            for i in range(depth)]
    new_mem_k = jnp.stack([kv[:, :d] for kv in kv_p]).reshape(depth, 1, mem_len, n_heads, d // n_heads)
    new_mem_v = jnp.stack([kv[:, d:] for kv in kv_p]).reshape(depth, 1, mem_len, n_heads, d // n_heads)

    pool_tails_p, pool_tails_s, conv_p, conv_s, gm_s = [], [], [], [], []
    for i in range(depth):
        slot = i // 3
        mixer = i % 3
        if mixer == 0:
            zero_state = jnp.zeros((1, HALO, d), F32)
            x, tail_p = _pool_mix(x, norm_mix[i], zero_state, pool_w, slot, pool_scale[slot], tile0=0,
                                  n_tiles=n_ptiles, tm=tm, seg_len=tm, carry=True, start_pos=0)
            st = jnp.zeros((b_s, HALO, d), F32).at[:, HALO - pool_state:].set(state_pool[slot])
            x, tail_s = _pool_mix(x, norm_mix[i], st, pool_w, slot, pool_scale[slot], tile0=n_ptiles,
                                  n_tiles=1, tm=tm, seg_len=dec_seq, carry=False,
                                  start_pos=PAST_LEN)
            pool_tails_p.append(tail_p[:, 16 - pool_state:])
            pool_tails_s.append(tail_s[:, 16 - pool_state:])
        elif mixer == 1:
            u = _norm_matmul(x, norm_mix[i], [(conv_w_in, (slot,), 0), (conv_w_in, (slot,), d)],
                             d, _glu, F32)
            n_slabs = d // LANES
            wb = _to_slabs(jnp.repeat(conv_w_dw[slot], SUBLANES, axis=0))
            bb = _to_slabs(jnp.broadcast_to(conv_b_dw[slot][None], (SUBLANES, d)))
            small = [wb, bb, conv_norm[slot].reshape(1, d)]
            small_specs = [_full_spec(wb.shape), _full_spec(bb.shape), _full_spec((1, d))]
            st_s = jnp.zeros((b_s, HALO, d), F32).at[:, HALO - conv_state:].set(state_conv[slot])
            for tile0, n_tiles, seg_len, carry, st in (
                    (0, n_ptiles, tm, True, jnp.zeros((1, n_slabs, HALO, LANES), F32)),
                    (n_ptiles, 1, dec_seq, False, _to_slabs(st_s))):
                (x,) = _prologue_matmul(
                    x, conv_w_out, (slot,), [u, st] + small,
                    [_tile_spec(tm, d, tile0), _full_spec(st.shape)] + small_specs,
                    functools.partial(_conv_prologue, tm=tm, seg_len=seg_len, carry=carry,
                                      width=conv_width),
                    name="conv_out_carry" if carry else "conv_out_state",
                    tile0=tile0, n_tiles=n_tiles, tm=tm, tn=tn_d,
                    scratch_shapes=[pltpu.VMEM((n_slabs, HALO + seg_len, LANES), F32),
                                    pltpu.VMEM((n_slabs, seg_len, LANES), F32)])
            conv_p.append(u[t_p - conv_state:t_p].reshape(1, conv_state, d))
            conv_s.append(u[t_p:].reshape(b_s, dec_seq, d)[:, dec_seq - conv_state:])
        else:
            z = _norm_matmul(x, norm_mix[i], [(gm_w_in, (slot,), 0)], 2 * d_gm, _gelu_tanh, BF16)
            tn_g = _pick(d, (512, 256, 128))
            for tile0, n_tiles, blk, emit_v in ((0, n_ptiles, min(seq, GM_BLOCK), False),
                                                (n_ptiles, 1, min(dec_seq, GM_BLOCK), True)):
                ws_big, bias = _gmlp_spatial_weights(gm_w_s[slot], gm_b_s[slot], blk, tm)
                outs = _prologue_matmul(
                    x, gm_w_out, (slot,), [z, gm_norm[slot].reshape(1, d_gm), ws_big, bias],
                    [_tile_spec(tm, 2 * d_gm, tile0), _full_spec((1, d_gm)),
                     _full_spec(ws_big.shape), _full_spec(bias.shape)],
                    functools.partial(_gmlp_prologue, tm=tm, emit_v=emit_v),
                    name="gmlp_out_state" if emit_v else "gmlp_out",
                    tile0=tile0, n_tiles=n_tiles, tm=tm, tn=tn_g,
                    extra_shapes=[jax.ShapeDtypeStruct((tm, d_gm), F32)] if emit_v else [],
                    extra_specs=[_full_spec((tm, d_gm))] if emit_v else [],
                    scratch_shapes=[pltpu.VMEM((tm, d_gm), BF16)])
                x = outs[0]
                if emit_v:
                    gm_s.append(outs[1].reshape(b_s, dec_seq, d_gm))

        q = _norm_matmul(x, norm_xattn[i], [(xa_w_q, (i,), 0)], d, _identity, BF16)
        kv = kv_p[i]
        k_p, v_p = bf(kv[:, :d]).reshape(1, mem_len, d), bf(kv[:, d:]).reshape(1, mem_len, d)
        k_s = bf(cache_mem_k[i]).reshape(b_s, mem_len, d)
        v_s = bf(cache_mem_v[i]).reshape(b_s, mem_len, d)
        half = tm // 2
        for tile0, n_tiles, kk, vv, segments in (
                (0, n_ptiles, k_p, v_p, ((0, half, 0), (half, half, 0))),
                (n_ptiles, 1, k_s, v_s, tuple((b * dec_seq, dec_seq, b) for b in range(b_s)))):
            (x,) = _prologue_matmul(
                x, xa_w_o, (i,), [q, kk, vv],
                [_tile_spec(tm, d, tile0), _full_spec(kk.shape), _full_spec(vv.shape)],
                functools.partial(_attn_prologue, segments=segments, n_heads=n_heads),
                name="attn_out_shared_kv" if tile0 == 0 else "attn_out_batched_kv",
                tile0=tile0, n_tiles=n_tiles, tm=tm, tn=d if tile0 == 0 else tn_d)

        j = i // 2
        if i % 2 == 0:
            t = _norm_matmul(x, norm_ffn[i], [(ffn_w1, (j,), 0), (ffn_w3, (j,), 0)], d_ff, _swiglu,
                             BF16)
            (x,) = _prologue_matmul(
                x, ffn_w2, (j,), [t], [_tile_spec(tm, d_ff, 0)], None,
                name="ffn_down", tile0=0, n_tiles=t_all // tm, tm=tm, tn=tn_d)
        else:
            x = _moe_ffn(x, norm_ffn[i], moe_router[j], moe_w1, moe_w3, moe_w2, j, norm_out, tm,
                         final_norm=(i == depth - 1))

    y_prompt = x[:t_p].reshape(1, seq, d)
    y_sample = x[t_p:].reshape(b_s, dec_seq, d)
    return (y_prompt, y_sample, new_mem_k, new_mem_v,
            jnp.stack(pool_tails_p), jnp.stack(pool_tails_s),
            jnp.stack(conv_p), jnp.stack(conv_s), jnp.stack(gm_s))
```

```python
import functools
import math

import jax
import jax.numpy as jnp
from jax import lax
from jax.experimental import pallas as pl
from jax.experimental.pallas import tpu as pltpu

F32 = jnp.float32
BF16 = jnp.bfloat16

EPS = 1e-6
PAST_LEN = 2048
CHUNK = 64
GM_BLOCK = 128
GM_GROUPS = 4
POOL_WINDOWS = (2, 4, 8, 16)
TOP_K = 2

TILE_M = 512
HALO = 32
PAD = 8
ROW_CHUNK = 16
SUBLANES = 8
LANES = 128
MXU_COLS = 256
VMEM_LIMIT = 60 * 1024 * 1024
NORM_MATMUL_VMEM_BUDGET = 48 * 1024 * 1024


def _params(n_grid):
    return pltpu.CompilerParams(
        dimension_semantics=("arbitrary",) * n_grid, vmem_limit_bytes=VMEM_LIMIT)


def _pick(n, prefs):
    for p in prefs:
        if n % p == 0:
            return p
    return n


def _stacked_spec(w, lead, block, index_fn):
    return pl.BlockSpec((None,) * len(lead) + tuple(block),
                        lambda *a: tuple(lead) + tuple(index_fn(*a)))


def _rms_rows(x_ref, g_ref, out_ref, inv_ref, n_rows, x_row0=0, out_row0=0):
    d = x_ref.shape[1]
    n_cc = d // LANES

    def stats(c, carry):
        r = pl.multiple_of(c * ROW_CHUNK, ROW_CHUNK)
        part = jnp.zeros((ROW_CHUNK, LANES), F32)
        for cc in range(n_cc):
            x = x_ref[pl.ds(x_row0 + r, ROW_CHUNK), cc * LANES:(cc + 1) * LANES].astype(F32)
            part = part + x * x
        ms = jnp.sum(part, axis=-1, keepdims=True) / d
        inv_ref[pl.ds(r, ROW_CHUNK), :] = jnp.broadcast_to(lax.rsqrt(ms + EPS), (ROW_CHUNK, LANES))
        return carry
    lax.fori_loop(0, n_rows // ROW_CHUNK, stats, 0, unroll=4)

    def scale(c, carry):
        r = pl.multiple_of(c * ROW_CHUNK, ROW_CHUNK)
        inv = inv_ref[pl.ds(r, ROW_CHUNK), :]
        for cc in range(n_cc):
            cols = slice(cc * LANES, (cc + 1) * LANES)
            x = x_ref[pl.ds(x_row0 + r, ROW_CHUNK), cols].astype(F32)
            out_ref[pl.ds(out_row0 + r, ROW_CHUNK), cols] = (
                (x * inv) * g_ref[:, cols]).astype(out_ref.dtype)
        return carry
    lax.fori_loop(0, n_rows // ROW_CHUNK, scale, 0, unroll=2)


def _cast_rows(x_ref, out_ref, n_rows):
    def body(c, carry):
        r = pl.multiple_of(c * ROW_CHUNK, ROW_CHUNK)
        out_ref[pl.ds(r, ROW_CHUNK), :] = x_ref[pl.ds(r, ROW_CHUNK), :].astype(out_ref.dtype)
        return carry
    lax.fori_loop(0, n_rows // ROW_CHUNK, body, 0, unroll=4)


def _sigmoid(x):
    return 1.0 / (1.0 + jnp.exp(-x))


def _silu(x):
    return x * _sigmoid(x)


def _gelu_tanh(x):
    c = math.sqrt(2.0 / math.pi)
    return 0.5 * x * (1.0 + jnp.tanh(c * (x + 0.044715 * (x * x * x))))


def _glu(a, gate):
    return a * _sigmoid(gate)


def _swiglu(a, b):
    return _silu(a) * b


def _identity(a):
    return a


def _norm_matmul_kernel(*refs, n_w, epilogue, tm):
    x_ref, g_ref = refs[0], refs[1]
    w_refs = refs[2:2 + n_w]
    o_ref = refs[2 + n_w]
    h_ref, inv_ref = refs[3 + n_w], refs[4 + n_w]

    @pl.when(pl.program_id(1) == 0)
    def _():
        _rms_rows(x_ref, g_ref, h_ref, inv_ref, tm)

    h = h_ref[...]
    accs = [jnp.dot(h, w[...], preferred_element_type=F32) for w in w_refs]
    o_ref[...] = epilogue(*accs).astype(o_ref.dtype)


def _norm_matmul_tiles(m, k, n_out, n_w, out_itemsize, col0s):
    for tm in (1536, 1024, 768, 512, 256, 128):
        for tn in (1024, 512, 256, 128):
            if m % tm or n_out % tn or any(c % tn for c in col0s):
                continue
            need = (2 * tm * k * 4 + tm * k * 2 + tm * LANES * 4
                    + n_w * 2 * k * tn * 2 + 2 * tm * tn * out_itemsize)
            if need <= NORM_MATMUL_VMEM_BUDGET:
                return tm, tn
    raise ValueError("no tile of the norm-matmul fits VMEM")


def _norm_matmul(x, g, ws, n_out, epilogue, out_dtype):
    m, k = x.shape
    tm, tn = _norm_matmul_tiles(m, k, n_out, len(ws), jnp.dtype(out_dtype).itemsize,
                                [c for _, _, c in ws])
    grid = (m // tm, n_out // tn)
    w_specs = [_stacked_spec(w, lead, (k, tn),
                             functools.partial(lambda i, j, cb: (0, j + cb), cb=c0 // tn))
               for w, lead, c0 in ws]
    return pl.pallas_call(
        functools.partial(_norm_matmul_kernel, n_w=len(ws), epilogue=epilogue, tm=tm),
        grid=grid,
        in_specs=[pl.BlockSpec((tm, k), lambda i, j: (i, 0)),
                  pl.BlockSpec((1, k), lambda i, j: (0, 0))] + w_specs,
        out_specs=pl.BlockSpec((tm, tn), lambda i, j: (i, j)),
        out_shape=jax.ShapeDtypeStruct((m, n_out), out_dtype),
        scratch_shapes=[pltpu.VMEM((tm, k), BF16), pltpu.VMEM((tm, LANES), F32)],
        compiler_params=_params(2),
        name="norm_matmul" + epilogue.__name__,
    )(x, g.reshape(1, k), *[w for w, _, _ in ws])


def _prologue_matmul_kernel(*refs, n_pro, n_extra, prologue):
    pro_refs = refs[:n_pro]
    w_ref, res_ref, o_ref = refs[n_pro], refs[n_pro + 1], refs[n_pro + 2]
    extra_refs = refs[n_pro + 3:n_pro + 3 + n_extra]
    scratch = refs[n_pro + 3 + n_extra:]
    a_ref = scratch[0]

    @pl.when(pl.program_id(1) == 0)
    def _():
        prologue(pro_refs, extra_refs, a_ref, scratch[1:])

    o_ref[...] = res_ref[...] + jnp.dot(a_ref[...], w_ref[...], preferred_element_type=F32)


def _residual_matmul_kernel(a_ref, w_ref, res_ref, o_ref):
    o_ref[...] = res_ref[...] + jnp.dot(a_ref[...], w_ref[...], preferred_element_type=F32)


def _residual_matmul(x_res, a, w, w_lead, tm, tn, name):
    m, k = a.shape
    n = w.shape[-1]
    return pl.pallas_call(
        _residual_matmul_kernel,
        grid=(n // tn, m // tm),
        in_specs=[pl.BlockSpec((tm, k), lambda j, i: (i, 0)),
                  _stacked_spec(w, w_lead, (k, tn), lambda j, i: (0, j)),
                  pl.BlockSpec((tm, tn), lambda j, i: (i, j))],
        out_specs=pl.BlockSpec((tm, tn), lambda j, i: (i, j)),
        out_shape=jax.ShapeDtypeStruct(x_res.shape, x_res.dtype),
        input_output_aliases={2: 0},
        compiler_params=_params(2),
        name=name,
    )(a, w, x_res)


def _prologue_matmul(x_res, w, w_lead, pro_inputs, pro_specs, prologue, *, name, tile0, n_tiles,
                     tm, tn, extra_shapes=(), extra_specs=(), scratch_shapes=()):
    k, n = w.shape[-2:]
    n_pro = len(pro_inputs)
    a_scratch = [pltpu.VMEM((tm, k), BF16)]
    return pl.pallas_call(
        functools.partial(_prologue_matmul_kernel, n_pro=n_pro, n_extra=len(extra_shapes),
                          prologue=prologue),
        grid=(n_tiles, n // tn),
        in_specs=list(pro_specs) + [
            _stacked_spec(w, w_lead, (k, tn), lambda i, j: (0, j)),
            pl.BlockSpec((tm, tn), lambda i, j: (i + tile0, j))],
        out_specs=[pl.BlockSpec((tm, tn), lambda i, j: (i + tile0, j))] + list(extra_specs),
        out_shape=[jax.ShapeDtypeStruct(x_res.shape, x_res.dtype)] + list(extra_shapes),
        scratch_shapes=a_scratch + list(scratch_shapes),
        input_output_aliases={n_pro + 1: 0},
        compiler_params=_params(2),
        name=name,
    )(*pro_inputs, w, x_res)


def _tile_spec(tm, width, tile0):
    return pl.BlockSpec((tm, width), lambda i, j: (i + tile0, 0))


def _full_spec(shape):
    return pl.BlockSpec(tuple(shape), lambda i, j: (0,) * len(shape))


def _pool_kernel(x_ref, g_ref, st_ref, w_ref, sc_ref, o_ref, tail_ref, hp_ref, sa_ref, sb_ref,
                 d_ref, inv_ref, *, tm, seg_len, carry, start_pos):
    d_model = x_ref.shape[1]
    n_groups = len(POOL_WINDOWS)
    gw = d_model // n_groups
    base = PAD + HALO
    n_seg = tm // seg_len
    i = pl.program_id(0)

    hp_ref[0:PAD, :] = jnp.zeros((PAD, d_model), F32)
    sa_ref[0:PAD, :] = jnp.zeros((PAD, gw), F32)
    sb_ref[0:PAD, :] = jnp.zeros((PAD, gw), F32)

    for s in range(n_seg):
        row0 = s * seg_len
        if carry:
            @pl.when(i == 0)
            def _():
                hp_ref[PAD:base, :] = jnp.zeros((HALO, d_model), F32)

            @pl.when(i > 0)
            def _():
                hp_ref[PAD:base, :] = hp_ref[PAD + seg_len:base + seg_len, :]
        else:
            hp_ref[PAD:base, :] = st_ref[s]

        _rms_rows(x_ref, g_ref, hp_ref, inv_ref, seg_len, x_row0=row0, out_row0=base)
        tail_ref[s] = hp_ref[base + seg_len - 16:base + seg_len, :]

        if carry:
            pos0 = start_pos + i * tm + row0
        else:
            pos0 = start_pos
        pos = (pos0 + lax.broadcasted_iota(jnp.int32, (seg_len, 1), 0)).astype(F32)

        n_steps = (HALO + seg_len) // 32
        for g, win in enumerate(POOL_WINDOWS):
            c0 = g * gw
            bufs = (sa_ref, sb_ref)
            n_stage = int(math.log2(win))
            for t in range(n_stage):
                shift = 1 << t
                dst = bufs[t % 2]
                for c in range(n_steps):
                    r = PAD + c * 32
                    if t == 0:
                        a = hp_ref[r:r + 32, c0:c0 + gw]
                        b = hp_ref[r - shift:r - shift + 32, c0:c0 + gw]
                    else:
                        src = bufs[(t - 1) % 2]
                        a = src[r:r + 32, :]
                        b = src[r - shift:r - shift + 32, :]
                    dst[r:r + 32, :] = a + b
            sums = bufs[(n_stage - 1) % 2]
            cnt = jnp.minimum(pos + 1.0, float(win))
            mean = sums[base:base + seg_len, :] / cnt
            d_ref[row0:row0 + seg_len, c0:c0 + gw] = (
                mean - hp_ref[base:base + seg_len, c0:c0 + gw]).astype(BF16)

    for g in range(n_groups):
        c0 = g * gw
        mix = jnp.dot(d_ref[:, c0:c0 + gw], w_ref[g], preferred_element_type=F32)
        o_ref[:, c0:c0 + gw] = x_ref[:, c0:c0 + gw] + mix * sc_ref[:, c0:c0 + gw]


def _pool_mix(x_all, g, state, w_bf, slot, scale, *, tile0, n_tiles, tm, seg_len, carry, start_pos):
    t_all, d = x_all.shape
    n_seg = tm // seg_len
    gw = d // len(POOL_WINDOWS)
    return pl.pallas_call(
        functools.partial(_pool_kernel, tm=tm, seg_len=seg_len, carry=carry, start_pos=start_pos),
        grid=(n_tiles,),
        in_specs=[pl.BlockSpec((tm, d), lambda i: (i + tile0, 0)),
                  pl.BlockSpec((1, d), lambda i: (0, 0)),
                  pl.BlockSpec(state.shape, lambda i: (0, 0, 0)),
                  _stacked_spec(w_bf, (slot,), w_bf.shape[1:], lambda i: (0, 0, 0)),
                  pl.BlockSpec((1, d), lambda i: (0, 0))],
        out_specs=[pl.BlockSpec((tm, d), lambda i: (i + tile0, 0)),
                   pl.BlockSpec((n_seg, 16, d), lambda i: (0, 0, 0))],
        out_shape=[jax.ShapeDtypeStruct((t_all, d), F32),
                   jax.ShapeDtypeStruct((n_seg, 16, d), F32)],
        scratch_shapes=[pltpu.VMEM((PAD + HALO + seg_len, d), F32),
                        pltpu.VMEM((PAD + HALO + seg_len, gw), F32),
                        pltpu.VMEM((PAD + HALO + seg_len, gw), F32),
                        pltpu.VMEM((tm, d), BF16),
                        pltpu.VMEM((seg_len, LANES), F32)],
        input_output_aliases={0: 0},
        compiler_params=_params(1),
        name="pool_mix_carry" if carry else "pool_mix_state",
    )(x_all, g.reshape(1, d), state, w_bf, scale.reshape(1, d))


CONV_STRIDE = 4


def _conv_prologue(pro_refs, extra_refs, a_ref, scratch, *, tm, seg_len, carry, width):
    u_ref, st_ref, wb_ref, bb_ref, gn_ref = pro_refs
    up_ref, c_ref = scratch
    d_model = u_ref.shape[1]
    n_slabs = d_model // LANES
    n_seg = tm // seg_len
    first = HALO - (width - 1)
    rows_per_step = SUBLANES * CONV_STRIDE
    i = pl.program_id(0)

    for s in range(n_seg):
        row0 = s * seg_len
        if carry:
            @pl.when(i == 0)
            def _():
                up_ref[:, 0:HALO, :] = jnp.zeros((n_slabs, HALO, LANES), F32)

            @pl.when(i > 0)
            def _():
                up_ref[:, 0:HALO, :] = up_ref[:, seg_len:seg_len + HALO, :]
        else:
            up_ref[:, 0:HALO, :] = st_ref[s]
        for sl in range(n_slabs):
            up_ref[sl, HALO:HALO + seg_len, :] = u_ref[row0:row0 + seg_len, sl * LANES:(sl + 1) * LANES]

        def slab_body(sl, cr):
            def step_body(c, cr2):
                r0 = pl.multiple_of(c * rows_per_step, rows_per_step)
                accs = [[bb_ref[sl], jnp.zeros((SUBLANES, LANES), F32)] for _ in range(CONV_STRIDE)]
                for m in range(width + CONV_STRIDE - 1):
                    x = up_ref[sl, pl.ds(r0 + first + m, SUBLANES, stride=CONV_STRIDE), :]
                    for ph in range(CONV_STRIDE):
                        k = m - ph
                        if 0 <= k < width:
                            accs[ph][k % 2] = accs[ph][k % 2] + (
                                x * wb_ref[sl, SUBLANES * k:SUBLANES * (k + 1), :])
                for ph in range(CONV_STRIDE):
                    c_ref[sl, pl.ds(r0 + ph, SUBLANES, stride=CONV_STRIDE), :] = accs[ph][0] + accs[ph][1]
                return cr2
            lax.fori_loop(0, seg_len // rows_per_step, step_body, 0, unroll=2)
            return cr
        lax.fori_loop(0, n_slabs, slab_body, 0)

        def norm_body(c, cr):
            r = pl.multiple_of(c * ROW_CHUNK, ROW_CHUNK)
            part = jnp.zeros((ROW_CHUNK, LANES), F32)
            for sl in range(n_slabs):
                v = c_ref[sl, pl.ds(r, ROW_CHUNK), :]
                part = part + v * v
            inv = lax.rsqrt(jnp.sum(part, axis=-1, keepdims=True) / d_model + EPS)
            for sl in range(n_slabs):
                cols = slice(sl * LANES, (sl + 1) * LANES)
                y = (c_ref[sl, pl.ds(r, ROW_CHUNK), :] * inv) * gn_ref[:, cols]
                a_ref[pl.ds(row0 + r, ROW_CHUNK), cols] = _silu(y).astype(BF16)
            return cr
        lax.fori_loop(0, seg_len // ROW_CHUNK, norm_body, 0, unroll=2)


def _to_slabs(a):
    *lead, rows, d = a.shape
    a = a.reshape(*lead, rows, d // LANES, LANES)
    return jnp.swapaxes(a, -3, -2)


GM_COLS = 512


def _gmlp_prologue(pro_refs, extra_refs, a_ref, scratch, *, tm, emit_v):
    z_ref, gv_ref, ws_ref, bs_ref = pro_refs
    (vn_ref,) = scratch
    d_gm = gv_ref.shape[1]
    gcols = d_gm // GM_GROUPS
    n_cc = d_gm // 1024

    def norm_body(c, cr):
        r = pl.multiple_of(c * ROW_CHUNK, ROW_CHUNK)
        ssq = jnp.zeros((ROW_CHUNK, 1), F32)
        for cc in range(n_cc):
            v = z_ref[pl.ds(r, ROW_CHUNK), d_gm + cc * 1024:d_gm + (cc + 1) * 1024].astype(F32)
            ssq = ssq + jnp.sum(v * v, axis=-1, keepdims=True)
        inv = lax.rsqrt(ssq / d_gm + EPS)
        for cc in range(n_cc):
            cols = slice(cc * 1024, (cc + 1) * 1024)
            v = z_ref[pl.ds(r, ROW_CHUNK), d_gm + cc * 1024:d_gm + (cc + 1) * 1024].astype(F32)
            vn = (v * inv) * gv_ref[:, cols]
            vn_ref[pl.ds(r, ROW_CHUNK), cols] = vn.astype(BF16)
            if emit_v:
                extra_refs[0][pl.ds(r, ROW_CHUNK), cols] = vn
        return cr
    lax.fori_loop(0, tm // ROW_CHUNK, norm_body, 0, unroll=2)

    for g in range(GM_GROUPS):
        for cc in range(gcols // GM_COLS):
            cols = slice(g * gcols + cc * GM_COLS, g * gcols + (cc + 1) * GM_COLS)
            s = jnp.dot(ws_ref[g], vn_ref[:, cols], preferred_element_type=F32)
            s = s + bs_ref[:, g:g + 1]
            a_ref[:, cols] = (z_ref[:, cols].astype(F32) * s).astype(BF16)


def _gmlp_spatial_weights(w_s, b_s, blk, tm):
    cidx = jnp.arange(blk) // CHUNK
    mask = cidx[None, :] <= cidx[:, None]
    ws = jnp.where(mask[None], w_s[:, :blk, :blk], 0.0)
    eye = jnp.eye(tm // blk, dtype=ws.dtype)
    big = jnp.einsum("ab,gij->gaibj", eye, ws).reshape(w_s.shape[0], tm, tm)
    bias = jnp.tile(b_s[:, :blk].T, (tm // blk, 1))
    return big.astype(BF16), bias.astype(F32)


def _attn_prologue(pro_refs, extra_refs, a_ref, scratch, *, segments, n_heads):
    q_ref, k_ref, v_ref = pro_refs
    d_model = q_ref.shape[1]
    hd = d_model // n_heads
    scale = hd ** -0.5
    for row0, n_rows, kv in segments:
        for h in range(n_heads):
            cols = slice(h * hd, (h + 1) * hd)
            q = q_ref[row0:row0 + n_rows, cols]
            s = lax.dot_general(q, k_ref[kv, :, cols], (((1,), (1,)), ((), ())),
                                preferred_element_type=F32) * scale
            m = jnp.max(s, axis=-1, keepdims=True)
            e = jnp.exp(s - m)
            p = e / jnp.sum(e, axis=-1, keepdims=True)
            o = jnp.dot(p.astype(BF16), v_ref[kv, :, cols], preferred_element_type=F32)
            a_ref[row0:row0 + n_rows, cols] = o.astype(BF16)


def _router_kernel(x_ref, g_ref, wr_ref, h_ref, meta_ref, gate_ref, cnt_ref, run_ref, inv_ref, *,
                   tm, n_experts):
    i = pl.program_id(0)

    @pl.when(i == 0)
    def _():
        run_ref[...] = jnp.zeros(run_ref.shape, F32)

    _rms_rows(x_ref, g_ref, h_ref, inv_ref, tm)
    logits = jnp.dot(h_ref[...], wr_ref[...], preferred_element_type=F32,
                     precision=lax.Precision.HIGHEST)
    lane = lax.broadcasted_iota(jnp.int32, (tm, LANES), 1)
    neg = jnp.float32(-jnp.inf)
    logits = jnp.where(lane < n_experts, logits, neg)
    m1 = jnp.max(logits, axis=-1, keepdims=True)
    i1 = jnp.min(jnp.where(logits == m1, lane, LANES), axis=-1, keepdims=True)
    rest = jnp.where(lane == i1, neg, logits)
    m2 = jnp.max(rest, axis=-1, keepdims=True)
    i2 = jnp.min(jnp.where(rest == m2, lane, LANES), axis=-1, keepdims=True)
    e2 = jnp.exp(m2 - m1)
    den = 1.0 + e2
    g1 = 1.0 / den
    g2 = e2 / den

    sel1 = lane == i1
    sel2 = lane == i2
    onehot = jnp.where(sel1 | sel2, 1.0, 0.0)
    r_i = lax.broadcasted_iota(jnp.int32, (tm, tm), 0)
    c_i = lax.broadcasted_iota(jnp.int32, (tm, tm), 1)
    tri = jnp.where(c_i < r_i, 1.0, 0.0).astype(BF16)
    before = jnp.dot(tri, onehot.astype(BF16), preferred_element_type=F32) + run_ref[0:1, :]
    rank1 = jnp.sum(jnp.where(sel1, before, 0.0), axis=-1, keepdims=True)
    rank2 = jnp.sum(jnp.where(sel2, before, 0.0), axis=-1, keepdims=True)
    run_ref[0:1, :] = run_ref[0:1, :] + jnp.sum(onehot, axis=0, keepdims=True)
    cnt_ref[...] = jnp.broadcast_to(run_ref[0:1, :], cnt_ref.shape)

    col = lax.broadcasted_iota(jnp.int32, (tm, 8), 1)
    meta_ref[...] = jnp.where(col == 0, i1.astype(F32),
                    jnp.where(col == 1, i2.astype(F32),
                    jnp.where(col == 2, rank1,
                    jnp.where(col == 3, rank2, 0.0))))
    gate_ref[:, 0:LANES] = jnp.broadcast_to(g1, (tm, LANES))
    gate_ref[:, LANES:2 * LANES] = jnp.broadcast_to(g2, (tm, LANES))


def _router(x_all, g, w_r, tm):
    t_all, d = x_all.shape
    n_experts = w_r.shape[1]
    wr_pad = jnp.zeros((d, LANES), F32).at[:, :n_experts].set(w_r)
    return pl.pallas_call(
        functools.partial(_router_kernel, tm=tm, n_experts=n_experts),
        grid=(t_all // tm,),
        in_specs=[pl.BlockSpec((tm, d), lambda i: (i, 0)),
                  pl.BlockSpec((1, d), lambda i: (0, 0)),
                  pl.BlockSpec((d, LANES), lambda i: (0, 0))],
        out_specs=[pl.BlockSpec((tm, d), lambda i: (i, 0)),
                   pl.BlockSpec((tm, 8), lambda i: (i, 0)),
                   pl.BlockSpec((tm, 2 * LANES), lambda i: (i, 0)),
                   pl.BlockSpec((8, LANES), lambda i: (0, 0))],
        out_shape=[jax.ShapeDtypeStruct((t_all, d), F32),
                   jax.ShapeDtypeStruct((t_all, 8), F32),
                   jax.ShapeDtypeStruct((t_all, 2 * LANES), F32),
                   jax.ShapeDtypeStruct((8, LANES), F32)],
        scratch_shapes=[pltpu.VMEM((8, LANES), F32), pltpu.VMEM((tm, LANES), F32)],
        compiler_params=_params(1),
        name="moe_router",
    )(x_all, g.reshape(1, d), wr_pad)


def _dispatch_kernel(p1_ref, p2_ref, ps_ref, pc_ref, nu_ref, h_ref, buf_ref, zero_ref, sem, *, tm,
                     n_experts, n_tiles):
    t0 = pl.program_id(0) * tm

    @pl.when(pl.program_id(0) == 0)
    def _():
        zero_ref[...] = jnp.zeros(zero_ref.shape, zero_ref.dtype)
        for e in range(n_experts):
            def pad_copy(r, e=e):
                return pltpu.make_async_copy(zero_ref.at[pl.ds(0, 1)],
                                             buf_ref.at[pl.ds(ps_ref[e] + r, 1)], sem)

            def pad_start(r, cr):
                pad_copy(r).start()
                return cr
            lax.fori_loop(0, pc_ref[e], pad_start, 0)

            def pad_wait(r, cr):
                pad_copy(r).wait()
                return cr
            lax.fori_loop(0, pc_ref[e], pad_wait, 0)

        def tile_copy(k):
            return pltpu.make_async_copy(zero_ref, buf_ref.at[pl.ds(k * tm, tm)], sem)

        def tile_start(k, cr):
            tile_copy(k).start()
            return cr
        lax.fori_loop(nu_ref[0], n_tiles, tile_start, 0)

        def tile_wait(k, cr):
            tile_copy(k).wait()
            return cr
        lax.fori_loop(nu_ref[0], n_tiles, tile_wait, 0)

    def copies(r):
        t = t0 + r
        return (pltpu.make_async_copy(h_ref.at[pl.ds(r, 1)], buf_ref.at[pl.ds(p1_ref[t], 1)], sem),
                pltpu.make_async_copy(h_ref.at[pl.ds(r, 1)], buf_ref.at[pl.ds(p2_ref[t], 1)], sem))

    def start(r, cr):
        for c in copies(r):
            c.start()
        return cr
    lax.fori_loop(0, tm, start, 0, unroll=8)

    def wait(r, cr):
        for c in copies(r):
            c.wait()
        return cr
    lax.fori_loop(0, tm, wait, 0, unroll=8)


def _dispatch(h, p1, p2, pad_start, pad_count, n_used, n_slots, tm):
    t_all, d = h.shape
    return pl.pallas_call(
        functools.partial(_dispatch_kernel, tm=tm, n_experts=pad_start.shape[0],
                          n_tiles=n_slots // tm),
        grid_spec=pltpu.PrefetchScalarGridSpec(
            num_scalar_prefetch=5,
            grid=(t_all // tm,),
            in_specs=[pl.BlockSpec((tm, d), lambda i, *_: (i, 0))],
            out_specs=pl.BlockSpec(memory_space=pl.ANY),
            scratch_shapes=[pltpu.VMEM((tm, d), h.dtype), pltpu.SemaphoreType.DMA(())]),
        out_shape=jax.ShapeDtypeStruct((n_slots, d), h.dtype),
        compiler_params=_params(1),
        name="moe_dispatch",
    )(p1, p2, pad_start, pad_count, n_used, h)


def _expert_up_kernel(te_ref, nu_ref, x_ref, w1_ref, w3_ref, o_ref, h_ref, *, tm, col_splits):
    i = pl.program_id(0)

    @pl.when(i < nu_ref[0])
    def _():
        _cast_rows(x_ref, h_ref, tm)
        h = h_ref[...]
        for c0, c1 in col_splits:
            a = jnp.dot(h, w1_ref[:, c0:c1], preferred_element_type=F32)
            b = jnp.dot(h, w3_ref[:, c0:c1], preferred_element_type=F32)
            o_ref[:, c0:c1] = _swiglu(a, b).astype(o_ref.dtype)

    @pl.when(i >= nu_ref[0])
    def _():
        o_ref[...] = jnp.zeros(o_ref.shape, o_ref.dtype)


def _expert_up(buf, w1, w3, layer, tile_expert, n_used, tm):
    n_slots, d = buf.shape
    f = w1.shape[-1]
    half = (f // MXU_COLS + 1) // 2 * MXU_COLS
    col_splits = ((0, half), (half, f)) if 0 < half < f else ((0, f),)
    w_spec = lambda: pl.BlockSpec((None, None, d, f), lambda i, te, nu: (layer, te[i], 0, 0),
                                  pipeline_mode=pl.Buffered(1))
    return pl.pallas_call(
        functools.partial(_expert_up_kernel, tm=tm, col_splits=col_splits),
        grid_spec=pltpu.PrefetchScalarGridSpec(
            num_scalar_prefetch=2,
            grid=(n_slots // tm,),
            in_specs=[pl.BlockSpec((tm, d), lambda i, te, nu: (i, 0)), w_spec(), w_spec()],
            out_specs=pl.BlockSpec((tm, f), lambda i, te, nu: (i, 0)),
            scratch_shapes=[pltpu.VMEM((tm, d), BF16)]),
        out_shape=jax.ShapeDtypeStruct((n_slots, f), BF16),
        compiler_params=_params(1),
        name="moe_expert_up",
    )(tile_expert, n_used, buf, w1, w3)


def _expert_down_kernel(te_ref, nu_ref, x_ref, w_ref, o_ref):
    i = pl.program_id(1)

    @pl.when(i < nu_ref[0])
    def _():
        o_ref[...] = jnp.dot(x_ref[...], w_ref[...], preferred_element_type=F32)

    @pl.when(i >= nu_ref[0])
    def _():
        o_ref[...] = jnp.zeros(o_ref.shape, o_ref.dtype)


def _expert_down(t_sorted, w2, layer, tile_expert, n_used, tm, tn):
    n_slots, f = t_sorted.shape
    d = w2.shape[-1]
    return pl.pallas_call(
        _expert_down_kernel,
        grid_spec=pltpu.PrefetchScalarGridSpec(
            num_scalar_prefetch=2,
            grid=(d // tn, n_slots // tm),
            in_specs=[pl.BlockSpec((tm, f), lambda j, i, te, nu: (i, 0)),
                      pl.BlockSpec((None, None, f, tn), lambda j, i, te, nu: (layer, te[i], 0, j))],
            out_specs=pl.BlockSpec((tm, tn), lambda j, i, te, nu: (i, j))),
        out_shape=jax.ShapeDtypeStruct((n_slots, d), F32),
        compiler_params=_params(2),
        name="moe_expert_down",
    )(tile_expert, n_used, t_sorted, w2)


def _combine_kernel(p1_ref, p2_ref, x_ref, gate_ref, y_ref, gout_ref, *rest, tm, n_tiles, final_norm):
    o_refs, (a_ref, b_ref, sems) = rest[:-3], rest[-3:]
    i = pl.program_id(0)
    slot = i % 2

    def copies(tile, sl, r):
        t = tile * tm + r
        return (pltpu.make_async_copy(y_ref.at[pl.ds(p1_ref[t], 1)], a_ref.at[sl, pl.ds(r, 1)],
                                      sems.at[sl]),
                pltpu.make_async_copy(y_ref.at[pl.ds(p2_ref[t], 1)], b_ref.at[sl, pl.ds(r, 1)],
                                      sems.at[sl]))

    def start_gather(tile, sl):
        def start(r, cr):
            for c in copies(tile, sl, r):
                c.start()
            return cr
        lax.fori_loop(0, tm, start, 0, unroll=8)

    @pl.when(i == 0)
    def _():
        start_gather(0, 0)

    @pl.when(i + 1 < n_tiles)
    def _():
        start_gather(i + 1, 1 - slot)

    def wait(r, cr):
        for c in copies(i, slot, r):
            c.wait()
        return cr
    lax.fori_loop(0, tm, wait, 0, unroll=8)

    d = x_ref.shape[1]
    n_cc = d // LANES

    def body(c, cr, o_ref):
        r = pl.multiple_of(c * ROW_CHUNK, ROW_CHUNK)
        rows = pl.ds(r, ROW_CHUNK)
        g1 = gate_ref[rows, 0:LANES]
        g2 = gate_ref[rows, LANES:2 * LANES]
        part = jnp.zeros((ROW_CHUNK, LANES), F32)
        for cc in range(n_cc):
            cols = slice(cc * LANES, (cc + 1) * LANES)
            x = x_ref[rows, cols] + (g1 * a_ref[slot, rows, cols] + g2 * b_ref[slot, rows, cols])
            o_ref[rows, cols] = x
            if final_norm:
                part = part + x * x
        if final_norm:
            ms = jnp.sum(part, axis=-1, keepdims=True) / d
            inv = jnp.broadcast_to(lax.rsqrt(ms + EPS), (ROW_CHUNK, LANES))
            for cc in range(n_cc):
                cols = slice(cc * LANES, (cc + 1) * LANES)
                o_ref[rows, cols] = (o_ref[rows, cols] * inv) * gout_ref[:, cols]
        return cr

    def run(o_ref):
        lax.fori_loop(0, tm // ROW_CHUNK, functools.partial(body, o_ref=o_ref), 0, unroll=2)

    if final_norm:
        @pl.when(i < n_tiles - 1)
        def _():
            run(o_refs[0])

        @pl.when(i == n_tiles - 1)
        def _():
            run(o_refs[1])
    else:
        run(o_refs[0])


def _combine(x_all, gates, y_sorted, p1, p2, g_out, tm, final_norm):
    t_all, d = x_all.shape
    n_tiles = t_all // tm
    if final_norm:
        out_specs = [pl.BlockSpec((tm, d), lambda i, a, b: (jnp.minimum(i, n_tiles - 2), 0)),
                     pl.BlockSpec((tm, d), lambda i, a, b: (0, 0))]
        out_shape = [jax.ShapeDtypeStruct((t_all - tm, d), F32), jax.ShapeDtypeStruct((tm, d), F32)]
    else:
        out_specs = pl.BlockSpec((tm, d), lambda i, a, b: (i, 0))
        out_shape = jax.ShapeDtypeStruct((t_all, d), F32)
    return pl.pallas_call(
        functools.partial(_combine_kernel, tm=tm, n_tiles=t_all // tm, final_norm=final_norm),
        grid_spec=pltpu.PrefetchScalarGridSpec(
            num_scalar_prefetch=2,
            grid=(t_all // tm,),
            in_specs=[pl.BlockSpec((tm, d), lambda i, a, b: (i, 0)),
                      pl.BlockSpec((tm, 2 * LANES), lambda i, a, b: (i, 0)),
                      pl.BlockSpec(memory_space=pl.ANY),
                      pl.BlockSpec((1, d), lambda i, a, b: (0, 0))],
            out_specs=out_specs,
            scratch_shapes=[pltpu.VMEM((2, tm, d), F32), pltpu.VMEM((2, tm, d), F32),
                            pltpu.SemaphoreType.DMA((2,))]),
        out_shape=out_shape,
        input_output_aliases={} if final_norm else {2: 0},
        compiler_params=_params(1),
        name="moe_combine_norm" if final_norm else "moe_combine",
    )(p1, p2, x_all, gates, y_sorted, g_out.reshape(1, d))


def _moe_ffn(x_all, g, w_r, w1, w3, w2, layer, g_out, tm, final_norm):
    t_all, d = x_all.shape
    n_experts = w_r.shape[1]
    h, meta, gates, counts = _router(x_all, g, w_r, tm)

    cnt = counts[0, :n_experts].astype(jnp.int32)
    padded = ((cnt + tm - 1) // tm) * tm
    ends = jnp.cumsum(padded)
    offs = ends - padded
    e1 = meta[:, 0].astype(jnp.int32)
    e2 = meta[:, 1].astype(jnp.int32)
    p1 = offs[e1] + meta[:, 2].astype(jnp.int32)
    p2 = offs[e2] + meta[:, 3].astype(jnp.int32)
    n_tiles = (TOP_K * t_all) // tm + n_experts
    tile_start = jnp.arange(n_tiles, dtype=jnp.int32) * tm
    tile_expert = jnp.minimum(
        jnp.sum((tile_start[:, None] >= ends[None, :]).astype(jnp.int32), axis=1), n_experts - 1)
    n_used = (ends[-1] // tm).reshape(1).astype(jnp.int32)

    buf = _dispatch(h, p1, p2, offs + cnt, padded - cnt, n_used, n_tiles * tm, tm)
    f = w1.shape[-1]
    t_sorted = _expert_up(buf, w1, w3, layer, tile_expert, n_used, tm)
    y_sorted = _expert_down(t_sorted, w2, layer, tile_expert, n_used, tm,
                            _pick(d, (1024, 512, 256, 128)))
    return _combine(x_all, gates, y_sorted, p1, p2, g_out, tm, final_norm)


def kernel(x_prompt, x_sample, cache_mem_k, cache_mem_v, state_pool, state_conv, mem_prompt,
           norm_mix, norm_xattn, norm_mem, norm_ffn, norm_out, pool_w, pool_scale,
           conv_w_in, conv_w_dw, conv_b_dw, conv_norm, conv_w_out,
           gm_w_in, gm_norm, gm_w_s, gm_b_s, gm_w_out, xa_w_q, xa_w_kv, xa_w_o,
           ffn_w1, ffn_w3, ffn_w2, moe_router, moe_w1, moe_w3, moe_w2):
    b_p, seq, d = x_prompt.shape
    b_s, dec_seq, _ = x_sample.shape
    depth = norm_mix.shape[0]
    mem_len = mem_prompt.shape[1]
    n_heads = cache_mem_k.shape[3]
    pool_state = state_pool.shape[2]
    conv_width = conv_w_dw.shape[1]
    conv_state = state_conv.shape[2]
    d_gm = gm_norm.shape[1]
    d_ff = ffn_w1.shape[2]
    tm = TILE_M
    assert b_p == 1 and seq % tm == 0 and b_s * dec_seq == tm
    assert conv_state <= HALO and pool_state <= 16 and seq >= conv_state
    assert dec_seq % (SUBLANES * CONV_STRIDE) == 0 and d % LANES == 0
    assert depth % 2 == 0, "the output norm is fused into the last layer's expert FFN"
    n_ptiles = seq // tm
    t_p = seq
    t_all = seq + tm

    x = jnp.concatenate([x_prompt.reshape(seq, d), x_sample.reshape(tm, d)], axis=0)
    bf = lambda a: a.astype(BF16)
    pool_w, conv_w_in, conv_w_out, gm_w_in, gm_w_out = map(
        bf, (pool_w, conv_w_in, conv_w_out, gm_w_in, gm_w_out))
    xa_w_q, xa_w_kv, xa_w_o, ffn_w1, ffn_w3, ffn_w2, moe_w1, moe_w3, moe_w2 = map(
        bf, (xa_w_q, xa_w_kv, xa_w_o, ffn_w1, ffn_w3, ffn_w2, moe_w1, moe_w3, moe_w2))

    tn_d = _pick(d, (1024, 512, 256, 128))
    k_cache = bf(cache_mem_k.reshape(depth, b_s, mem_len, d))
    v_cache = bf(cache_mem_v.reshape(depth, b_s, mem_len, d))

    mem = mem_prompt.reshape(mem_len, d)
    kv_p = [_norm_matmul(mem, norm_mem[i], [(xa_w_kv, (i,), 0)], 2 * d, _identity, F32)
            for i in range(depth)]
    new_mem_k = jnp.stack([kv[:, :d] for kv in kv_p]).reshape(depth, 1, mem_len, n_heads, d // n_heads)
    new_mem_v = jnp.stack([kv[:, d:] for kv in kv_p]).reshape(depth, 1, mem_len, n_heads, d // n_heads)

    pool_tails_p, pool_tails_s, conv_p, conv_s, gm_s = [], [], [], [], []
    for i in range(depth):
        slot = i // 3
        mixer = i % 3
        if mixer == 0:
            zero_state = jnp.zeros((1, HALO, d), F32)
            x, tail_p = _pool_mix(x, norm_mix[i], zero_state, pool_w, slot, pool_scale[slot], tile0=0,
                                  n_tiles=n_ptiles, tm=tm, seg_len=tm, carry=True, start_pos=0)
            st = jnp.zeros((b_s, HALO, d), F32).at[:, HALO - pool_state:].set(state_pool[slot])
            x, tail_s = _pool_mix(x, norm_mix[i], st, pool_w, slot, pool_scale[slot], tile0=n_ptiles,
                                  n_tiles=1, tm=tm, seg_len=dec_seq, carry=False,
                                  start_pos=PAST_LEN)
            pool_tails_p.append(tail_p[:, 16 - pool_state:])
            pool_tails_s.append(tail_s[:, 16 - pool_state:])
        elif mixer == 1:
            u = _norm_matmul(x, norm_mix[i], [(conv_w_in, (slot,), 0), (conv_w_in, (slot,), d)],
                             d, _glu, F32)
            n_slabs = d // LANES
            wb = _to_slabs(jnp.repeat(conv_w_dw[slot], SUBLANES, axis=0))
            bb = _to_slabs(jnp.broadcast_to(conv_b_dw[slot][None], (SUBLANES, d)))
            small = [wb, bb, conv_norm[slot].reshape(1, d)]
            small_specs = [_full_spec(wb.shape), _full_spec(bb.shape), _full_spec((1, d))]
            st_s = jnp.zeros((b_s, HALO, d), F32).at[:, HALO - conv_state:].set(state_conv[slot])
            for tile0, n_tiles, seg_len, carry, st in (
                    (0, n_ptiles, tm, True, jnp.zeros((1, n_slabs, HALO, LANES), F32)),
                    (n_ptiles, 1, dec_seq, False, _to_slabs(st_s))):
                (x,) = _prologue_matmul(
                    x, conv_w_out, (slot,), [u, st] + small,
                    [_tile_spec(tm, d, tile0), _full_spec(st.shape)] + small_specs,
                    functools.partial(_conv_prologue, tm=tm, seg_len=seg_len, carry=carry,
                                      width=conv_width),
                    name="conv_out_carry" if carry else "conv_out_state",
                    tile0=tile0, n_tiles=n_tiles, tm=tm, tn=tn_d,
                    scratch_shapes=[pltpu.VMEM((n_slabs, HALO + seg_len, LANES), F32),
                                    pltpu.VMEM((n_slabs, seg_len, LANES), F32)])
            conv_p.append(u[t_p - conv_state:t_p].reshape(1, conv_state, d))
            conv_s.append(u[t_p:].reshape(b_s, dec_seq, d)[:, dec_seq - conv_state:])
        else:
            z = _norm_matmul(x, norm_mix[i], [(gm_w_in, (slot,), 0)], 2 * d_gm, _gelu_tanh, BF16)
            tn_g = _pick(d, (512, 256, 128))
            for tile0, n_tiles, blk, emit_v in ((0, n_ptiles, min(seq, GM_BLOCK), False),
                                                (n_ptiles, 1, min(dec_seq, GM_BLOCK), True)):
                ws_big, bias = _gmlp_spatial_weights(gm_w_s[slot], gm_b_s[slot], blk, tm)
                outs = _prologue_matmul(
                    x, gm_w_out, (slot,), [z, gm_norm[slot].reshape(1, d_gm), ws_big, bias],
                    [_tile_spec(tm, 2 * d_gm, tile0), _full_spec((1, d_gm)),
                     _full_spec(ws_big.shape), _full_spec(bias.shape)],
                    functools.partial(_gmlp_prologue, tm=tm, emit_v=emit_v),
                    name="gmlp_out_state" if emit_v else "gmlp_out",
                    tile0=tile0, n_tiles=n_tiles, tm=tm, tn=tn_g,
                    extra_shapes=[jax.ShapeDtypeStruct((tm, d_gm), F32)] if emit_v else [],
                    extra_specs=[_full_spec((tm, d_gm))] if emit_v else [],
                    scratch_shapes=[pltpu.VMEM((tm, d_gm), BF16)])
                x = outs[0]
                if emit_v:
                    gm_s.append(outs[1].reshape(b_s, dec_seq, d_gm))

        q = _norm_matmul(x, norm_xattn[i], [(xa_w_q, (i,), 0)], d, _identity, BF16)
        kv = kv_p[i]
        k_p, v_p = bf(kv[:, :d]).reshape(1, 1, mem_len, d), bf(kv[:, d:]).reshape(1, 1, mem_len, d)
        half = tm // 2
        for tile0, n_tiles, kk, vv, lead, segments in (
                (0, n_ptiles, k_p, v_p, 0, ((0, half, 0), (half, half, 0))),
                (n_ptiles, 1, k_cache, v_cache, i,
                 tuple((b * dec_seq, dec_seq, b) for b in range(b_s)))):
            kv_spec = lambda a: _stacked_spec(a, (lead,), a.shape[1:], lambda i_, j_: (0, 0, 0))
            (x,) = _prologue_matmul(
                x, xa_w_o, (i,), [q, kk, vv],
                [_tile_spec(tm, d, tile0), kv_spec(kk), kv_spec(vv)],
                functools.partial(_attn_prologue, segments=segments, n_heads=n_heads),
                name="attn_out_shared_kv" if tile0 == 0 else "attn_out_batched_kv",
                tile0=tile0, n_tiles=n_tiles, tm=tm, tn=d if tile0 == 0 else tn_d)

        j = i // 2
        if i % 2 == 0:
            t = _norm_matmul(x, norm_ffn[i], [(ffn_w1, (j,), 0), (ffn_w3, (j,), 0)], d_ff, _swiglu,
                             BF16)
            x = _residual_matmul(x, t, ffn_w2, (j,), tm, tn_d, "ffn_down")
        else:
            x = _moe_ffn(x, norm_ffn[i], moe_router[j], moe_w1, moe_w3, moe_w2, j, norm_out, tm,
                         final_norm=(i == depth - 1))

    y_prompt = x[0].reshape(1, seq, d)
    y_sample = x[1].reshape(b_s, dec_seq, d)
    return (y_prompt, y_sample, new_mem_k, new_mem_v,
            jnp.stack(pool_tails_p), jnp.stack(pool_tails_s),
            jnp.stack(conv_p), jnp.stack(conv_s), jnp.stack(gm_s))
```

```python
import functools
import math

import jax
import jax.numpy as jnp
from jax import lax
from jax.experimental import pallas as pl
from jax.experimental.pallas import tpu as pltpu

F32 = jnp.float32
BF16 = jnp.bfloat16

EPS = 1e-6
PAST_LEN = 2048
CHUNK = 64
GM_BLOCK = 128
GM_GROUPS = 4
POOL_WINDOWS = (2, 4, 8, 16)
TOP_K = 2

TILE_M = 512
HALO = 32
PAD = 8
ROW_CHUNK = 16
SUBLANES = 8
LANES = 128
MXU_COLS = 256
VMEM_LIMIT = 60 * 1024 * 1024
NORM_MATMUL_VMEM_BUDGET = 48 * 1024 * 1024


def _params(n_grid):
    return pltpu.CompilerParams(
        dimension_semantics=("arbitrary",) * n_grid, vmem_limit_bytes=VMEM_LIMIT)


def _pick(n, prefs):
    for p in prefs:
        if n % p == 0:
            return p
    return n


def _stacked_spec(w, lead, block, index_fn):
    return pl.BlockSpec((None,) * len(lead) + tuple(block),
                        lambda *a: tuple(lead) + tuple(index_fn(*a)))


def _rms_rows(x_ref, g_ref, out_ref, inv_ref, n_rows, x_row0=0, out_row0=0):
    d = x_ref.shape[1]
    n_cc = d // LANES

    def stats(c, carry):
        r = pl.multiple_of(c * ROW_CHUNK, ROW_CHUNK)
        part = jnp.zeros((ROW_CHUNK, LANES), F32)
        for cc in range(n_cc):
            x = x_ref[pl.ds(x_row0 + r, ROW_CHUNK), cc * LANES:(cc + 1) * LANES].astype(F32)
            part = part + x * x
        ms = jnp.sum(part, axis=-1, keepdims=True) / d
        inv_ref[pl.ds(r, ROW_CHUNK), :] = jnp.broadcast_to(lax.rsqrt(ms + EPS), (ROW_CHUNK, LANES))
        return carry
    lax.fori_loop(0, n_rows // ROW_CHUNK, stats, 0, unroll=4)

    def scale(c, carry):
        r = pl.multiple_of(c * ROW_CHUNK, ROW_CHUNK)
        inv = inv_ref[pl.ds(r, ROW_CHUNK), :]
        for cc in range(n_cc):
            cols = slice(cc * LANES, (cc + 1) * LANES)
            x = x_ref[pl.ds(x_row0 + r, ROW_CHUNK), cols].astype(F32)
            out_ref[pl.ds(out_row0 + r, ROW_CHUNK), cols] = (
                (x * inv) * g_ref[:, cols]).astype(out_ref.dtype)
        return carry
    lax.fori_loop(0, n_rows // ROW_CHUNK, scale, 0, unroll=2)


def _cast_rows(x_ref, out_ref, n_rows):
    def body(c, carry):
        r = pl.multiple_of(c * ROW_CHUNK, ROW_CHUNK)
        out_ref[pl.ds(r, ROW_CHUNK), :] = x_ref[pl.ds(r, ROW_CHUNK), :].astype(out_ref.dtype)
        return carry
    lax.fori_loop(0, n_rows // ROW_CHUNK, body, 0, unroll=4)


def _sigmoid(x):
    return 1.0 / (1.0 + jnp.exp(-x))


def _silu(x):
    return x * _sigmoid(x)


def _gelu_tanh(x):
    c = math.sqrt(2.0 / math.pi)
    return 0.5 * x * (1.0 + jnp.tanh(c * (x + 0.044715 * (x * x * x))))


def _glu(a, gate):
    return a * _sigmoid(gate)


def _swiglu(a, b):
    return _silu(a) * b


def _identity(a):
    return a


def _norm_matmul_kernel(*refs, n_w, epilogue, tm):
    x_ref, g_ref = refs[0], refs[1]
    w_refs = refs[2:2 + n_w]
    o_ref = refs[2 + n_w]
    h_ref, inv_ref = refs[3 + n_w], refs[4 + n_w]

    @pl.when(pl.program_id(1) == 0)
    def _():
        _rms_rows(x_ref, g_ref, h_ref, inv_ref, tm)

    h = h_ref[...]
    accs = [jnp.dot(h, w[...], preferred_element_type=F32) for w in w_refs]
    o_ref[...] = epilogue(*accs).astype(o_ref.dtype)


def _norm_matmul_tiles(m, k, n_out, n_w, out_itemsize, col0s):
    for tm in (1536, 1024, 768, 512, 256, 128):
        for tn in (1024, 512, 256, 128):
            if m % tm or n_out % tn or any(c % tn for c in col0s):
                continue
            need = (2 * tm * k * 4 + tm * k * 2 + tm * LANES * 4
                    + n_w * 2 * k * tn * 2 + 2 * tm * tn * out_itemsize)
            if need <= NORM_MATMUL_VMEM_BUDGET:
                return tm, tn
    raise ValueError("no tile of the norm-matmul fits VMEM")


def _norm_matmul(x, g, ws, n_out, epilogue, out_dtype):
    m, k = x.shape
    tm, tn = _norm_matmul_tiles(m, k, n_out, len(ws), jnp.dtype(out_dtype).itemsize,
                                [c for _, _, c in ws])
    grid = (m // tm, n_out // tn)
    w_specs = [_stacked_spec(w, lead, (k, tn),
                             functools.partial(lambda i, j, cb: (0, j + cb), cb=c0 // tn))
               for w, lead, c0 in ws]
    return pl.pallas_call(
        functools.partial(_norm_matmul_kernel, n_w=len(ws), epilogue=epilogue, tm=tm),
        grid=grid,
        in_specs=[pl.BlockSpec((tm, k), lambda i, j: (i, 0)),
                  pl.BlockSpec((1, k), lambda i, j: (0, 0))] + w_specs,
        out_specs=pl.BlockSpec((tm, tn), lambda i, j: (i, j)),
        out_shape=jax.ShapeDtypeStruct((m, n_out), out_dtype),
        scratch_shapes=[pltpu.VMEM((tm, k), BF16), pltpu.VMEM((tm, LANES), F32)],
        compiler_params=_params(2),
        name="norm_matmul" + epilogue.__name__,
    )(x, g.reshape(1, k), *[w for w, _, _ in ws])


def _prologue_matmul_kernel(*refs, n_pro, n_extra, prologue):
    pro_refs = refs[:n_pro]
    w_ref, res_ref, o_ref = refs[n_pro], refs[n_pro + 1], refs[n_pro + 2]
    extra_refs = refs[n_pro + 3:n_pro + 3 + n_extra]
    scratch = refs[n_pro + 3 + n_extra:]
    a_ref = scratch[0]

    @pl.when(pl.program_id(1) == 0)
    def _():
        prologue(pro_refs, extra_refs, a_ref, scratch[1:])

    o_ref[...] = res_ref[...] + jnp.dot(a_ref[...], w_ref[...], preferred_element_type=F32)


def _residual_matmul_kernel(a_ref, w_ref, res_ref, o_ref):
    o_ref[...] = res_ref[...] + jnp.dot(a_ref[...], w_ref[...], preferred_element_type=F32)


def _residual_matmul(x_res, a, w, w_lead, tm, tn, name):
    m, k = a.shape
    n = w.shape[-1]
    return pl.pallas_call(
        _residual_matmul_kernel,
        grid=(n // tn, m // tm),
        in_specs=[pl.BlockSpec((tm, k), lambda j, i: (i, 0)),
                  _stacked_spec(w, w_lead, (k, tn), lambda j, i: (0, j)),
                  pl.BlockSpec((tm, tn), lambda j, i: (i, j))],
        out_specs=pl.BlockSpec((tm, tn), lambda j, i: (i, j)),
        out_shape=jax.ShapeDtypeStruct(x_res.shape, x_res.dtype),
        input_output_aliases={2: 0},
        compiler_params=_params(2),
        name=name,
    )(a, w, x_res)


def _prologue_matmul(x_res, w, w_lead, pro_inputs, pro_specs, prologue, *, name, tile0, n_tiles,
                     tm, tn, extra_shapes=(), extra_specs=(), scratch_shapes=()):
    k, n = w.shape[-2:]
    n_pro = len(pro_inputs)
    a_scratch = [pltpu.VMEM((tm, k), BF16)]
    return pl.pallas_call(
        functools.partial(_prologue_matmul_kernel, n_pro=n_pro, n_extra=len(extra_shapes),
                          prologue=prologue),
        grid=(n_tiles, n // tn),
        in_specs=list(pro_specs) + [
            _stacked_spec(w, w_lead, (k, tn), lambda i, j: (0, j)),
            pl.BlockSpec((tm, tn), lambda i, j: (i + tile0, j))],
        out_specs=[pl.BlockSpec((tm, tn), lambda i, j: (i + tile0, j))] + list(extra_specs),
        out_shape=[jax.ShapeDtypeStruct(x_res.shape, x_res.dtype)] + list(extra_shapes),
        scratch_shapes=a_scratch + list(scratch_shapes),
        input_output_aliases={n_pro + 1: 0},
        compiler_params=_params(2),
        name=name,
    )(*pro_inputs, w, x_res)


def _tile_spec(tm, width, tile0):
    return pl.BlockSpec((tm, width), lambda i, j: (i + tile0, 0))


def _full_spec(shape):
    return pl.BlockSpec(tuple(shape), lambda i, j: (0,) * len(shape))


def _pool_kernel(x_ref, g_ref, st_ref, w_ref, sc_ref, o_ref, tail_ref, hp_ref, sa_ref, sb_ref,
                 d_ref, inv_ref, *, tm, seg_len, carry, start_pos):
    d_model = x_ref.shape[1]
    n_groups = len(POOL_WINDOWS)
    gw = d_model // n_groups
    base = PAD + HALO
    n_seg = tm // seg_len
    i = pl.program_id(0)

    hp_ref[0:PAD, :] = jnp.zeros((PAD, d_model), F32)
    sa_ref[0:PAD, :] = jnp.zeros((PAD, gw), F32)
    sb_ref[0:PAD, :] = jnp.zeros((PAD, gw), F32)

    for s in range(n_seg):
        row0 = s * seg_len
        if carry:
            @pl.when(i == 0)
            def _():
                hp_ref[PAD:base, :] = jnp.zeros((HALO, d_model), F32)

            @pl.when(i > 0)
            def _():
                hp_ref[PAD:base, :] = hp_ref[PAD + seg_len:base + seg_len, :]
        else:
            hp_ref[PAD:base, :] = st_ref[s]

        _rms_rows(x_ref, g_ref, hp_ref, inv_ref, seg_len, x_row0=row0, out_row0=base)
        tail_ref[s] = hp_ref[base + seg_len - 16:base + seg_len, :]

        if carry:
            pos0 = start_pos + i * tm + row0
        else:
            pos0 = start_pos
        pos = (pos0 + lax.broadcasted_iota(jnp.int32, (seg_len, 1), 0)).astype(F32)

        n_steps = (HALO + seg_len) // 32
        for g, win in enumerate(POOL_WINDOWS):
            c0 = g * gw
            bufs = (sa_ref, sb_ref)
            n_stage = int(math.log2(win))
            for t in range(n_stage):
                shift = 1 << t
                dst = bufs[t % 2]
                for c in range(n_steps):
                    r = PAD + c * 32
                    if t == 0:
                        a = hp_ref[r:r + 32, c0:c0 + gw]
                        b = hp_ref[r - shift:r - shift + 32, c0:c0 + gw]
                    else:
                        src = bufs[(t - 1) % 2]
                        a = src[r:r + 32, :]
                        b = src[r - shift:r - shift + 32, :]
                    dst[r:r + 32, :] = a + b
            sums = bufs[(n_stage - 1) % 2]
            cnt = jnp.minimum(pos + 1.0, float(win))
            mean = sums[base:base + seg_len, :] / cnt
            d_ref[row0:row0 + seg_len, c0:c0 + gw] = (
                mean - hp_ref[base:base + seg_len, c0:c0 + gw]).astype(BF16)

    for g in range(n_groups):
        c0 = g * gw
        mix = jnp.dot(d_ref[:, c0:c0 + gw], w_ref[g], preferred_element_type=F32)
        o_ref[:, c0:c0 + gw] = x_ref[:, c0:c0 + gw] + mix * sc_ref[:, c0:c0 + gw]


def _pool_mix(x_all, g, state, w_bf, slot, scale, *, tile0, n_tiles, tm, seg_len, carry, start_pos):
    t_all, d = x_all.shape
    n_seg = tm // seg_len
    gw = d // len(POOL_WINDOWS)
    return pl.pallas_call(
        functools.partial(_pool_kernel, tm=tm, seg_len=seg_len, carry=carry, start_pos=start_pos),
        grid=(n_tiles,),
        in_specs=[pl.BlockSpec((tm, d), lambda i: (i + tile0, 0)),
                  pl.BlockSpec((1, d), lambda i: (0, 0)),
                  pl.BlockSpec(state.shape, lambda i: (0, 0, 0)),
                  _stacked_spec(w_bf, (slot,), w_bf.shape[1:], lambda i: (0, 0, 0)),
                  pl.BlockSpec((1, d), lambda i: (0, 0))],
        out_specs=[pl.BlockSpec((tm, d), lambda i: (i + tile0, 0)),
                   pl.BlockSpec((n_seg, 16, d), lambda i: (0, 0, 0))],
        out_shape=[jax.ShapeDtypeStruct((t_all, d), F32),
                   jax.ShapeDtypeStruct((n_seg, 16, d), F32)],
        scratch_shapes=[pltpu.VMEM((PAD + HALO + seg_len, d), F32),
                        pltpu.VMEM((PAD + HALO + seg_len, gw), F32),
                        pltpu.VMEM((PAD + HALO + seg_len, gw), F32),
                        pltpu.VMEM((tm, d), BF16),
                        pltpu.VMEM((seg_len, LANES), F32)],
        input_output_aliases={0: 0},
        compiler_params=_params(1),
        name="pool_mix_carry" if carry else "pool_mix_state",
    )(x_all, g.reshape(1, d), state, w_bf, scale.reshape(1, d))


CONV_STRIDE = 4


def _conv_prologue(pro_refs, extra_refs, a_ref, scratch, *, tm, seg_len, carry, width):
    u_ref, st_ref, wb_ref, bb_ref, gn_ref = pro_refs
    up_ref, c_ref = scratch
    d_model = u_ref.shape[1]
    n_slabs = d_model // LANES
    n_seg = tm // seg_len
    first = HALO - (width - 1)
    rows_per_step = SUBLANES * CONV_STRIDE
    i = pl.program_id(0)

    for s in range(n_seg):
        row0 = s * seg_len
        if carry:
            @pl.when(i == 0)
            def _():
                up_ref[:, 0:HALO, :] = jnp.zeros((n_slabs, HALO, LANES), F32)

            @pl.when(i > 0)
            def _():
                up_ref[:, 0:HALO, :] = up_ref[:, seg_len:seg_len + HALO, :]
        else:
            up_ref[:, 0:HALO, :] = st_ref[s]
        for sl in range(n_slabs):
            up_ref[sl, HALO:HALO + seg_len, :] = u_ref[row0:row0 + seg_len, sl * LANES:(sl + 1) * LANES]

        def slab_body(sl, cr):
            def step_body(c, cr2):
                r0 = pl.multiple_of(c * rows_per_step, rows_per_step)
                accs = [[bb_ref[sl], jnp.zeros((SUBLANES, LANES), F32)] for _ in range(CONV_STRIDE)]
                for m in range(width + CONV_STRIDE - 1):
                    x = up_ref[sl, pl.ds(r0 + first + m, SUBLANES, stride=CONV_STRIDE), :]
                    for ph in range(CONV_STRIDE):
                        k = m - ph
                        if 0 <= k < width:
                            accs[ph][k % 2] = accs[ph][k % 2] + (
                                x * wb_ref[sl, SUBLANES * k:SUBLANES * (k + 1), :])
                for ph in range(CONV_STRIDE):
                    c_ref[sl, pl.ds(r0 + ph, SUBLANES, stride=CONV_STRIDE), :] = accs[ph][0] + accs[ph][1]
                return cr2
            lax.fori_loop(0, seg_len // rows_per_step, step_body, 0, unroll=2)
            return cr
        lax.fori_loop(0, n_slabs, slab_body, 0)

        def norm_body(c, cr):
            r = pl.multiple_of(c * ROW_CHUNK, ROW_CHUNK)
            part = jnp.zeros((ROW_CHUNK, LANES), F32)
            for sl in range(n_slabs):
                v = c_ref[sl, pl.ds(r, ROW_CHUNK), :]
                part = part + v * v
            inv = lax.rsqrt(jnp.sum(part, axis=-1, keepdims=True) / d_model + EPS)
            for sl in range(n_slabs):
                cols = slice(sl * LANES, (sl + 1) * LANES)
                y = (c_ref[sl, pl.ds(r, ROW_CHUNK), :] * inv) * gn_ref[:, cols]
                a_ref[pl.ds(row0 + r, ROW_CHUNK), cols] = _silu(y).astype(BF16)
            return cr
        lax.fori_loop(0, seg_len // ROW_CHUNK, norm_body, 0, unroll=2)


def _to_slabs(a):
    *lead, rows, d = a.shape
    a = a.reshape(*lead, rows, d // LANES, LANES)
    return jnp.swapaxes(a, -3, -2)


GM_COLS = 512


def _gmlp_prologue(pro_refs, extra_refs, a_ref, scratch, *, tm, emit_v):
    z_ref, gv_ref, ws_ref, bs_ref = pro_refs
    (vn_ref,) = scratch
    d_gm = gv_ref.shape[1]
    gcols = d_gm // GM_GROUPS
    n_cc = d_gm // 1024

    def norm_body(c, cr):
        r = pl.multiple_of(c * ROW_CHUNK, ROW_CHUNK)
        ssq = jnp.zeros((ROW_CHUNK, 1), F32)
        for cc in range(n_cc):
            v = z_ref[pl.ds(r, ROW_CHUNK), d_gm + cc * 1024:d_gm + (cc + 1) * 1024].astype(F32)
            ssq = ssq + jnp.sum(v * v, axis=-1, keepdims=True)
        inv = lax.rsqrt(ssq / d_gm + EPS)
        for cc in range(n_cc):
            cols = slice(cc * 1024, (cc + 1) * 1024)
            v = z_ref[pl.ds(r, ROW_CHUNK), d_gm + cc * 1024:d_gm + (cc + 1) * 1024].astype(F32)
            vn = (v * inv) * gv_ref[:, cols]
            vn_ref[pl.ds(r, ROW_CHUNK), cols] = vn.astype(BF16)
            if emit_v:
                extra_refs[0][pl.ds(r, ROW_CHUNK), cols] = vn
        return cr
    lax.fori_loop(0, tm // ROW_CHUNK, norm_body, 0, unroll=2)

    for g in range(GM_GROUPS):
        for cc in range(gcols // GM_COLS):
            cols = slice(g * gcols + cc * GM_COLS, g * gcols + (cc + 1) * GM_COLS)
            s = jnp.dot(ws_ref[g], vn_ref[:, cols], preferred_element_type=F32)
            s = s + bs_ref[:, g:g + 1]
            a_ref[:, cols] = (z_ref[:, cols].astype(F32) * s).astype(BF16)


def _gmlp_spatial_weights(w_s, b_s, blk, tm):
    cidx = jnp.arange(blk) // CHUNK
    mask = cidx[None, :] <= cidx[:, None]
    ws = jnp.where(mask[None], w_s[:, :blk, :blk], 0.0)
    eye = jnp.eye(tm // blk, dtype=ws.dtype)
    big = jnp.einsum("ab,gij->gaibj", eye, ws).reshape(w_s.shape[0], tm, tm)
    bias = jnp.tile(b_s[:, :blk].T, (tm // blk, 1))
    return big.astype(BF16), bias.astype(F32)


def _attn_prologue(pro_refs, extra_refs, a_ref, scratch, *, segments, n_heads):
    q_ref, k_ref, v_ref = pro_refs
    d_model = q_ref.shape[1]
    hd = d_model // n_heads
    scale = hd ** -0.5
    for row0, n_rows, kv in segments:
        for h in range(n_heads):
            cols = slice(h * hd, (h + 1) * hd)
            q = q_ref[row0:row0 + n_rows, cols]
            s = lax.dot_general(q, k_ref[kv, :, cols], (((1,), (1,)), ((), ())),
                                preferred_element_type=F32) * scale
            m = jnp.max(s, axis=-1, keepdims=True)
            e = jnp.exp(s - m)
            p = e / jnp.sum(e, axis=-1, keepdims=True)
            o = jnp.dot(p.astype(BF16), v_ref[kv, :, cols], preferred_element_type=F32)
            a_ref[row0:row0 + n_rows, cols] = o.astype(BF16)


def _router_kernel(x_ref, g_ref, wr_ref, h_ref, meta_ref, gate_ref, cnt_ref, run_ref, inv_ref, *,
                   tm, n_experts):
    i = pl.program_id(0)

    @pl.when(i == 0)
    def _():
        run_ref[...] = jnp.zeros(run_ref.shape, F32)

    _rms_rows(x_ref, g_ref, h_ref, inv_ref, tm)
    logits = jnp.dot(h_ref[...], wr_ref[...], preferred_element_type=F32,
                     precision=lax.Precision.HIGHEST)
    lane = lax.broadcasted_iota(jnp.int32, (tm, LANES), 1)
    neg = jnp.float32(-jnp.inf)
    logits = jnp.where(lane < n_experts, logits, neg)
    m1 = jnp.max(logits, axis=-1, keepdims=True)
    i1 = jnp.min(jnp.where(logits == m1, lane, LANES), axis=-1, keepdims=True)
    rest = jnp.where(lane == i1, neg, logits)
    m2 = jnp.max(rest, axis=-1, keepdims=True)
    i2 = jnp.min(jnp.where(rest == m2, lane, LANES), axis=-1, keepdims=True)
    e2 = jnp.exp(m2 - m1)
    den = 1.0 + e2
    g1 = 1.0 / den
    g2 = e2 / den

    sel1 = lane == i1
    sel2 = lane == i2
    onehot = jnp.where(sel1 | sel2, 1.0, 0.0)
    r_i = lax.broadcasted_iota(jnp.int32, (tm, tm), 0)
    c_i = lax.broadcasted_iota(jnp.int32, (tm, tm), 1)
    tri = jnp.where(c_i < r_i, 1.0, 0.0).astype(BF16)
    before = jnp.dot(tri, onehot.astype(BF16), preferred_element_type=F32) + run_ref[0:1, :]
    rank1 = jnp.sum(jnp.where(sel1, before, 0.0), axis=-1, keepdims=True)
    rank2 = jnp.sum(jnp.where(sel2, before, 0.0), axis=-1, keepdims=True)
    run_ref[0:1, :] = run_ref[0:1, :] + jnp.sum(onehot, axis=0, keepdims=True)
    cnt_ref[...] = jnp.broadcast_to(run_ref[0:1, :], cnt_ref.shape)

    col = lax.broadcasted_iota(jnp.int32, (tm, 8), 1)
    meta_ref[...] = jnp.where(col == 0, i1.astype(F32),
                    jnp.where(col == 1, i2.astype(F32),
                    jnp.where(col == 2, rank1,
                    jnp.where(col == 3, rank2, 0.0))))
    gate_ref[:, 0:LANES] = jnp.broadcast_to(g1, (tm, LANES))
    gate_ref[:, LANES:2 * LANES] = jnp.broadcast_to(g2, (tm, LANES))


def _router(x_all, g, w_r, tm):
    t_all, d = x_all.shape
    n_experts = w_r.shape[1]
    wr_pad = jnp.zeros((d, LANES), F32).at[:, :n_experts].set(w_r)
    return pl.pallas_call(
        functools.partial(_router_kernel, tm=tm, n_experts=n_experts),
        grid=(t_all // tm,),
        in_specs=[pl.BlockSpec((tm, d), lambda i: (i, 0)),
                  pl.BlockSpec((1, d), lambda i: (0, 0)),
                  pl.BlockSpec((d, LANES), lambda i: (0, 0))],
        out_specs=[pl.BlockSpec((tm, d), lambda i: (i, 0)),
                   pl.BlockSpec((tm, 8), lambda i: (i, 0)),
                   pl.BlockSpec((tm, 2 * LANES), lambda i: (i, 0)),
                   pl.BlockSpec((8, LANES), lambda i: (0, 0))],
        out_shape=[jax.ShapeDtypeStruct((t_all, d), F32),
                   jax.ShapeDtypeStruct((t_all, 8), F32),
                   jax.ShapeDtypeStruct((t_all, 2 * LANES), F32),
                   jax.ShapeDtypeStruct((8, LANES), F32)],
        scratch_shapes=[pltpu.VMEM((8, LANES), F32), pltpu.VMEM((tm, LANES), F32)],
        compiler_params=_params(1),
        name="moe_router",
    )(x_all, g.reshape(1, d), wr_pad)


def _dispatch_kernel(p1_ref, p2_ref, ps_ref, pc_ref, nu_ref, h_ref, buf_ref, zero_ref, sem, *, tm,
                     n_experts, n_tiles):
    t0 = pl.program_id(0) * tm

    @pl.when(pl.program_id(0) == 0)
    def _():
        zero_ref[...] = jnp.zeros(zero_ref.shape, zero_ref.dtype)
        for e in range(n_experts):
            def pad_copy(r, e=e):
                return pltpu.make_async_copy(zero_ref.at[pl.ds(0, 1)],
                                             buf_ref.at[pl.ds(ps_ref[e] + r, 1)], sem)

            def pad_start(r, cr):
                pad_copy(r).start()
                return cr
            lax.fori_loop(0, pc_ref[e], pad_start, 0)

            def pad_wait(r, cr):
                pad_copy(r).wait()
                return cr
            lax.fori_loop(0, pc_ref[e], pad_wait, 0)

        def tile_copy(k):
            return pltpu.make_async_copy(zero_ref, buf_ref.at[pl.ds(k * tm, tm)], sem)

        def tile_start(k, cr):
            tile_copy(k).start()
            return cr
        lax.fori_loop(nu_ref[0], n_tiles, tile_start, 0)

        def tile_wait(k, cr):
            tile_copy(k).wait()
            return cr
        lax.fori_loop(nu_ref[0], n_tiles, tile_wait, 0)

    def copies(r):
        t = t0 + r
        return (pltpu.make_async_copy(h_ref.at[pl.ds(r, 1)], buf_ref.at[pl.ds(p1_ref[t], 1)], sem),
                pltpu.make_async_copy(h_ref.at[pl.ds(r, 1)], buf_ref.at[pl.ds(p2_ref[t], 1)], sem))

    def start(r, cr):
        for c in copies(r):
            c.start()
        return cr
    lax.fori_loop(0, tm, start, 0, unroll=8)

    def wait(r, cr):
        for c in copies(r):
            c.wait()
        return cr
    lax.fori_loop(0, tm, wait, 0, unroll=8)


def _dispatch(h, p1, p2, pad_start, pad_count, n_used, n_slots, tm):
    t_all, d = h.shape
    return pl.pallas_call(
        functools.partial(_dispatch_kernel, tm=tm, n_experts=pad_start.shape[0],
                          n_tiles=n_slots // tm),
        grid_spec=pltpu.PrefetchScalarGridSpec(
            num_scalar_prefetch=5,
            grid=(t_all // tm,),
            in_specs=[pl.BlockSpec((tm, d), lambda i, *_: (i, 0))],
            out_specs=pl.BlockSpec(memory_space=pl.ANY),
            scratch_shapes=[pltpu.VMEM((tm, d), h.dtype), pltpu.SemaphoreType.DMA(())]),
        out_shape=jax.ShapeDtypeStruct((n_slots, d), h.dtype),
        compiler_params=_params(1),
        name="moe_dispatch",
    )(p1, p2, pad_start, pad_count, n_used, h)


def _expert_up_kernel(te_ref, nu_ref, x_ref, w1_ref, w3_ref, o_ref, h_ref, *, tm, col_splits):
    i = pl.program_id(0)

    @pl.when(i < nu_ref[0])
    def _():
        _cast_rows(x_ref, h_ref, tm)
        h = h_ref[...]
        for c0, c1 in col_splits:
            a = jnp.dot(h, w1_ref[:, c0:c1], preferred_element_type=F32)
            b = jnp.dot(h, w3_ref[:, c0:c1], preferred_element_type=F32)
            o_ref[:, c0:c1] = _swiglu(a, b).astype(o_ref.dtype)

    @pl.when(i >= nu_ref[0])
    def _():
        o_ref[...] = jnp.zeros(o_ref.shape, o_ref.dtype)


def _expert_up(buf, w1, w3, layer, tile_expert, n_used, tm):
    n_slots, d = buf.shape
    f = w1.shape[-1]
    half = (f // MXU_COLS + 1) // 2 * MXU_COLS
    col_splits = ((0, half), (half, f)) if 0 < half < f else ((0, f),)
    w_spec = lambda: pl.BlockSpec((None, None, d, f), lambda i, te, nu: (layer, te[i], 0, 0),
                                  pipeline_mode=pl.Buffered(1))
    return pl.pallas_call(
        functools.partial(_expert_up_kernel, tm=tm, col_splits=col_splits),
        grid_spec=pltpu.PrefetchScalarGridSpec(
            num_scalar_prefetch=2,
            grid=(n_slots // tm,),
            in_specs=[pl.BlockSpec((tm, d), lambda i, te, nu: (i, 0)), w_spec(), w_spec()],
            out_specs=pl.BlockSpec((tm, f), lambda i, te, nu: (i, 0)),
            scratch_shapes=[pltpu.VMEM((tm, d), BF16)]),
        out_shape=jax.ShapeDtypeStruct((n_slots, f), BF16),
        compiler_params=_params(1),
        name="moe_expert_up",
    )(tile_expert, n_used, buf, w1, w3)


def _expert_down_kernel(te_ref, nu_ref, x_ref, w_ref, o_ref, wb_ref, *, f):
    i = pl.program_id(1)
    used = i < nu_ref[0]
    new_block = jnp.logical_or(i == 0, te_ref[i] != te_ref[jnp.maximum(i - 1, 0)])

    @pl.when(jnp.logical_and(used, new_block))
    def _():
        _cast_rows(w_ref, wb_ref, f)

    @pl.when(used)
    def _():
        o_ref[...] = jnp.dot(x_ref[...], wb_ref[...], preferred_element_type=F32)

    @pl.when(jnp.logical_not(used))
    def _():
        o_ref[...] = jnp.zeros(o_ref.shape, o_ref.dtype)


def _expert_down(t_sorted, w2, layer, tile_expert, n_used, tm, tn):
    n_slots, f = t_sorted.shape
    d = w2.shape[-1]
    return pl.pallas_call(
        functools.partial(_expert_down_kernel, f=f),
        grid_spec=pltpu.PrefetchScalarGridSpec(
            num_scalar_prefetch=2,
            grid=(d // tn, n_slots // tm),
            in_specs=[pl.BlockSpec((tm, f), lambda j, i, te, nu: (i, 0)),
                      pl.BlockSpec((None, None, f, tn), lambda j, i, te, nu: (layer, te[i], 0, j))],
            out_specs=pl.BlockSpec((tm, tn), lambda j, i, te, nu: (i, j)),
            scratch_shapes=[pltpu.VMEM((f, tn), BF16)]),
        out_shape=jax.ShapeDtypeStruct((n_slots, d), F32),
        compiler_params=_params(2),
        name="moe_expert_down",
    )(tile_expert, n_used, t_sorted, w2)


def _combine_kernel(p1_ref, p2_ref, x_ref, gate_ref, y_ref, gout_ref, *rest, tm, n_tiles, final_norm):
    o_refs, (a_ref, b_ref, sems) = rest[:-3], rest[-3:]
    i = pl.program_id(0)
    slot = i % 2

    def copies(tile, sl, r):
        t = tile * tm + r
        return (pltpu.make_async_copy(y_ref.at[pl.ds(p1_ref[t], 1)], a_ref.at[sl, pl.ds(r, 1)],
                                      sems.at[sl]),
                pltpu.make_async_copy(y_ref.at[pl.ds(p2_ref[t], 1)], b_ref.at[sl, pl.ds(r, 1)],
                                      sems.at[sl]))

    def start_gather(tile, sl):
        def start(r, cr):
            for c in copies(tile, sl, r):
                c.start()
            return cr
        lax.fori_loop(0, tm, start, 0, unroll=8)

    @pl.when(i == 0)
    def _():
        start_gather(0, 0)

    @pl.when(i + 1 < n_tiles)
    def _():
        start_gather(i + 1, 1 - slot)

    def wait(r, cr):
        for c in copies(i, slot, r):
            c.wait()
        return cr
    lax.fori_loop(0, tm, wait, 0, unroll=8)

    d = x_ref.shape[1]
    n_cc = d // LANES

    def body(c, cr, o_ref):
        r = pl.multiple_of(c * ROW_CHUNK, ROW_CHUNK)
        rows = pl.ds(r, ROW_CHUNK)
        g1 = gate_ref[rows, 0:LANES]
        g2 = gate_ref[rows, LANES:2 * LANES]
        part = jnp.zeros((ROW_CHUNK, LANES), F32)
        for cc in range(n_cc):
            cols = slice(cc * LANES, (cc + 1) * LANES)
            x = x_ref[rows, cols] + (g1 * a_ref[slot, rows, cols] + g2 * b_ref[slot, rows, cols])
            o_ref[rows, cols] = x
            if final_norm:
                part = part + x * x
        if final_norm:
            ms = jnp.sum(part, axis=-1, keepdims=True) / d
            inv = jnp.broadcast_to(lax.rsqrt(ms + EPS), (ROW_CHUNK, LANES))
            for cc in range(n_cc):
                cols = slice(cc * LANES, (cc + 1) * LANES)
                o_ref[rows, cols] = (o_ref[rows, cols] * inv) * gout_ref[:, cols]
        return cr

    def run(o_ref):
        lax.fori_loop(0, tm // ROW_CHUNK, functools.partial(body, o_ref=o_ref), 0, unroll=2)

    if final_norm:
        @pl.when(i < n_tiles - 1)
        def _():
            run(o_refs[0])

        @pl.when(i == n_tiles - 1)
        def _():
            run(o_refs[1])
    else:
        run(o_refs[0])


def _combine(x_all, gates, y_sorted, p1, p2, g_out, tm, final_norm):
    t_all, d = x_all.shape
    n_tiles = t_all // tm
    if final_norm:
        out_specs = [pl.BlockSpec((tm, d), lambda i, a, b: (jnp.minimum(i, n_tiles - 2), 0)),
                     pl.BlockSpec((tm, d), lambda i, a, b: (0, 0))]
        out_shape = [jax.ShapeDtypeStruct((t_all - tm, d), F32), jax.ShapeDtypeStruct((tm, d), F32)]
    else:
        out_specs = pl.BlockSpec((tm, d), lambda i, a, b: (i, 0))
        out_shape = jax.ShapeDtypeStruct((t_all, d), F32)
    return pl.pallas_call(
        functools.partial(_combine_kernel, tm=tm, n_tiles=t_all // tm, final_norm=final_norm),
        grid_spec=pltpu.PrefetchScalarGridSpec(
            num_scalar_prefetch=2,
            grid=(t_all // tm,),
            in_specs=[pl.BlockSpec((tm, d), lambda i, a, b: (i, 0)),
                      pl.BlockSpec((tm, 2 * LANES), lambda i, a, b: (i, 0)),
                      pl.BlockSpec(memory_space=pl.ANY),
                      pl.BlockSpec((1, d), lambda i, a, b: (0, 0))],
            out_specs=out_specs,
            scratch_shapes=[pltpu.VMEM((2, tm, d), F32), pltpu.VMEM((2, tm, d), F32),
                            pltpu.SemaphoreType.DMA((2,))]),
        out_shape=out_shape,
        input_output_aliases={} if final_norm else {2: 0},
        compiler_params=_params(1),
        name="moe_combine_norm" if final_norm else "moe_combine",
    )(p1, p2, x_all, gates, y_sorted, g_out.reshape(1, d))


def _moe_ffn(x_all, g, w_r, w1, w3, w2, layer, g_out, tm, final_norm):
    t_all, d = x_all.shape
    n_experts = w_r.shape[1]
    h, meta, gates, counts = _router(x_all, g, w_r, tm)

    cnt = counts[0, :n_experts].astype(jnp.int32)
    padded = ((cnt + tm - 1) // tm) * tm
    ends = jnp.cumsum(padded)
    offs = ends - padded
    e1 = meta[:, 0].astype(jnp.int32)
    e2 = meta[:, 1].astype(jnp.int32)
    p1 = offs[e1] + meta[:, 2].astype(jnp.int32)
    p2 = offs[e2] + meta[:, 3].astype(jnp.int32)
    n_tiles = (TOP_K * t_all) // tm + n_experts
    tile_start = jnp.arange(n_tiles, dtype=jnp.int32) * tm
    tile_expert = jnp.minimum(
        jnp.sum((tile_start[:, None] >= ends[None, :]).astype(jnp.int32), axis=1), n_experts - 1)
    n_used = (ends[-1] // tm).reshape(1).astype(jnp.int32)

    buf = _dispatch(h, p1, p2, offs + cnt, padded - cnt, n_used, n_tiles * tm, tm)
    f = w1.shape[-1]
    t_sorted = _expert_up(buf, w1, w3, layer, tile_expert, n_used, tm)
    y_sorted = _expert_down(t_sorted, w2, layer, tile_expert, n_used, tm,
                            _pick(d, (1024, 512, 256, 128)))
    return _combine(x_all, gates, y_sorted, p1, p2, g_out, tm, final_norm)


def kernel(x_prompt, x_sample, cache_mem_k, cache_mem_v, state_pool, state_conv, mem_prompt,
           norm_mix, norm_xattn, norm_mem, norm_ffn, norm_out, pool_w, pool_scale,
           conv_w_in, conv_w_dw, conv_b_dw, conv_norm, conv_w_out,
           gm_w_in, gm_norm, gm_w_s, gm_b_s, gm_w_out, xa_w_q, xa_w_kv, xa_w_o,
           ffn_w1, ffn_w3, ffn_w2, moe_router, moe_w1, moe_w3, moe_w2):
    b_p, seq, d = x_prompt.shape
    b_s, dec_seq, _ = x_sample.shape
    depth = norm_mix.shape[0]
    mem_len = mem_prompt.shape[1]
    n_heads = cache_mem_k.shape[3]
    pool_state = state_pool.shape[2]
    conv_width = conv_w_dw.shape[1]
    conv_state = state_conv.shape[2]
    d_gm = gm_norm.shape[1]
    d_ff = ffn_w1.shape[2]
    tm = TILE_M
    assert b_p == 1 and seq % tm == 0 and b_s * dec_seq == tm
    assert conv_state <= HALO and pool_state <= 16 and seq >= conv_state
    assert dec_seq % (SUBLANES * CONV_STRIDE) == 0 and d % LANES == 0
    assert depth % 2 == 0, "the output norm is fused into the last layer's expert FFN"
    n_ptiles = seq // tm
    t_p = seq
    t_all = seq + tm

    x = jnp.concatenate([x_prompt.reshape(seq, d), x_sample.reshape(tm, d)], axis=0)
    bf = lambda a: a.astype(BF16)
    pool_w, conv_w_in, conv_w_out, gm_w_in, gm_w_out = map(
        bf, (pool_w, conv_w_in, conv_w_out, gm_w_in, gm_w_out))
    xa_w_q, xa_w_kv, xa_w_o, ffn_w1, ffn_w3, ffn_w2, moe_w1, moe_w3 = map(
        bf, (xa_w_q, xa_w_kv, xa_w_o, ffn_w1, ffn_w3, ffn_w2, moe_w1, moe_w3))

    tn_d = _pick(d, (1024, 512, 256, 128))
    k_cache = bf(cache_mem_k.reshape(depth, b_s, mem_len, d))
    v_cache = bf(cache_mem_v.reshape(depth, b_s, mem_len, d))

    mem = mem_prompt.reshape(mem_len, d)
    kv_p = [_norm_matmul(mem, norm_mem[i], [(xa_w_kv, (i,), 0)], 2 * d, _identity, F32)
            for i in range(depth)]
    new_mem_k = jnp.stack([kv[:, :d] for kv in kv_p]).reshape(depth, 1, mem_len, n_heads, d // n_heads)
    new_mem_v = jnp.stack([kv[:, d:] for kv in kv_p]).reshape(depth, 1, mem_len, n_heads, d // n_heads)

    pool_tails_p, pool_tails_s, conv_p, conv_s, gm_s = [], [], [], [], []
    for i in range(depth):
        slot = i // 3
        mixer = i % 3
        if mixer == 0:
            zero_state = jnp.zeros((1, HALO, d), F32)
            x, tail_p = _pool_mix(x, norm_mix[i], zero_state, pool_w, slot, pool_scale[slot], tile0=0,
                                  n_tiles=n_ptiles, tm=tm, seg_len=tm, carry=True, start_pos=0)
            st = jnp.zeros((b_s, HALO, d), F32).at[:, HALO - pool_state:].set(state_pool[slot])
            x, tail_s = _pool_mix(x, norm_mix[i], st, pool_w, slot, pool_scale[slot], tile0=n_ptiles,
                                  n_tiles=1, tm=tm, seg_len=dec_seq, carry=False,
                                  start_pos=PAST_LEN)
            pool_tails_p.append(tail_p[:, 16 - pool_state:])
            pool_tails_s.append(tail_s[:, 16 - pool_state:])
        elif mixer == 1:
            u = _norm_matmul(x, norm_mix[i], [(conv_w_in, (slot,), 0), (conv_w_in, (slot,), d)],
                             d, _glu, F32)
            n_slabs = d // LANES
            wb = _to_slabs(jnp.repeat(conv_w_dw[slot], SUBLANES, axis=0))
            bb = _to_slabs(jnp.broadcast_to(conv_b_dw[slot][None], (SUBLANES, d)))
            small = [wb, bb, conv_norm[slot].reshape(1, d)]
            small_specs = [_full_spec(wb.shape), _full_spec(bb.shape), _full_spec((1, d))]
            st_s = jnp.zeros((b_s, HALO, d), F32).at[:, HALO - conv_state:].set(state_conv[slot])
            for tile0, n_tiles, seg_len, carry, st in (
                    (0, n_ptiles, tm, True, jnp.zeros((1, n_slabs, HALO, LANES), F32)),
                    (n_ptiles, 1, dec_seq, False, _to_slabs(st_s))):
                (x,) = _prologue_matmul(
                    x, conv_w_out, (slot,), [u, st] + small,
                    [_tile_spec(tm, d, tile0), _full_spec(st.shape)] + small_specs,
                    functools.partial(_conv_prologue, tm=tm, seg_len=seg_len, carry=carry,
                                      width=conv_width),
                    name="conv_out_carry" if carry else "conv_out_state",
                    tile0=tile0, n_tiles=n_tiles, tm=tm, tn=tn_d,
                    scratch_shapes=[pltpu.VMEM((n_slabs, HALO + seg_len, LANES), F32),
                                    pltpu.VMEM((n_slabs, seg_len, LANES), F32)])
            conv_p.append(u[t_p - conv_state:t_p].reshape(1, conv_state, d))
            conv_s.append(u[t_p:].reshape(b_s, dec_seq, d)[:, dec_seq - conv_state:])
        else:
            z = _norm_matmul(x, norm_mix[i], [(gm_w_in, (slot,), 0)], 2 * d_gm, _gelu_tanh, BF16)
            tn_g = _pick(d, (512, 256, 128))
            for tile0, n_tiles, blk, emit_v in ((0, n_ptiles, min(seq, GM_BLOCK), False),
                                                (n_ptiles, 1, min(dec_seq, GM_BLOCK), True)):
                ws_big, bias = _gmlp_spatial_weights(gm_w_s[slot], gm_b_s[slot], blk, tm)
                outs = _prologue_matmul(
                    x, gm_w_out, (slot,), [z, gm_norm[slot].reshape(1, d_gm), ws_big, bias],
                    [_tile_spec(tm, 2 * d_gm, tile0), _full_spec((1, d_gm)),
                     _full_spec(ws_big.shape), _full_spec(bias.shape)],
                    functools.partial(_gmlp_prologue, tm=tm, emit_v=emit_v),
                    name="gmlp_out_state" if emit_v else "gmlp_out",
                    tile0=tile0, n_tiles=n_tiles, tm=tm, tn=tn_g,
                    extra_shapes=[jax.ShapeDtypeStruct((tm, d_gm), F32)] if emit_v else [],
                    extra_specs=[_full_spec((tm, d_gm))] if emit_v else [],
                    scratch_shapes=[pltpu.VMEM((tm, d_gm), BF16)])
                x = outs[0]
                if emit_v:
                    gm_s.append(outs[1].reshape(b_s, dec_seq, d_gm))

        q = _norm_matmul(x, norm_xattn[i], [(xa_w_q, (i,), 0)], d, _identity, BF16)
        kv = kv_p[i]
        k_p, v_p = bf(kv[:, :d]).reshape(1, 1, mem_len, d), bf(kv[:, d:]).reshape(1, 1, mem_len, d)
        half = tm // 2
        for tile0, n_tiles, kk, vv, lead, segments in (
                (0, n_ptiles, k_p, v_p, 0, ((0, half, 0), (half, half, 0))),
                (n_ptiles, 1, k_cache, v_cache, i,
                 tuple((b * dec_seq, dec_seq, b) for b in range(b_s)))):
            kv_spec = lambda a: _stacked_spec(a, (lead,), a.shape[1:], lambda i_, j_: (0, 0, 0))
            (x,) = _prologue_matmul(
                x, xa_w_o, (i,), [q, kk, vv],
                [_tile_spec(tm, d, tile0), kv_spec(kk), kv_spec(vv)],
                functools.partial(_attn_prologue, segments=segments, n_heads=n_heads),
                name="attn_out_shared_kv" if tile0 == 0 else "attn_out_batched_kv",
                tile0=tile0, n_tiles=n_tiles, tm=tm, tn=d if tile0 == 0 else tn_d)

        j = i // 2
        if i % 2 == 0:
            t = _norm_matmul(x, norm_ffn[i], [(ffn_w1, (j,), 0), (ffn_w3, (j,), 0)], d_ff, _swiglu,
                             BF16)
            x = _residual_matmul(x, t, ffn_w2, (j,), tm, tn_d, "ffn_down")
        else:
            x = _moe_ffn(x, norm_ffn[i], moe_router[j], moe_w1, moe_w3, moe_w2, j, norm_out, tm,
                         final_norm=(i == depth - 1))

    y_prompt = x[0].reshape(1, seq, d)
    y_sample = x[1].reshape(b_s, dec_seq, d)
    return (y_prompt, y_sample, new_mem_k, new_mem_v,
            jnp.stack(pool_tails_p), jnp.stack(pool_tails_s),
            jnp.stack(conv_p), jnp.stack(conv_s), jnp.stack(gm_s))
```

```python
import functools
import math

import jax
import jax.numpy as jnp
from jax import lax
from jax.experimental import pallas as pl
from jax.experimental.pallas import tpu as pltpu

F32 = jnp.float32
BF16 = jnp.bfloat16

EPS = 1e-6
PAST_LEN = 2048
CHUNK = 64
GM_BLOCK = 128
GM_GROUPS = 4
POOL_WINDOWS = (2, 4, 8, 16)
TOP_K = 2

TILE_M = 512
HALO = 32
PAD = 8
ROW_CHUNK = 16
SUBLANES = 8
LANES = 128
MXU_COLS = 256
VMEM_LIMIT = 60 * 1024 * 1024
NORM_MATMUL_VMEM_BUDGET = 48 * 1024 * 1024


def _params(n_grid):
    return pltpu.CompilerParams(
        dimension_semantics=("arbitrary",) * n_grid, vmem_limit_bytes=VMEM_LIMIT)


def _pick(n, prefs):
    for p in prefs:
        if n % p == 0:
            return p
    return n


def _stacked_spec(w, lead, block, index_fn):
    return pl.BlockSpec((None,) * len(lead) + tuple(block),
                        lambda *a: tuple(lead) + tuple(index_fn(*a)))


def _rms_rows(x_ref, g_ref, out_ref, inv_ref, n_rows, x_row0=0, out_row0=0):
    d = x_ref.shape[1]
    n_cc = d // LANES

    def stats(c, carry):
        r = pl.multiple_of(c * ROW_CHUNK, ROW_CHUNK)
        parts = [jnp.zeros((ROW_CHUNK, LANES), F32), jnp.zeros((ROW_CHUNK, LANES), F32)]
        for cc in range(n_cc):
            x = x_ref[pl.ds(x_row0 + r, ROW_CHUNK), cc * LANES:(cc + 1) * LANES].astype(F32)
            parts[cc % 2] = parts[cc % 2] + x * x
        ms = jnp.sum(parts[0] + parts[1], axis=-1, keepdims=True) / d
        inv_ref[pl.ds(r, ROW_CHUNK), :] = jnp.broadcast_to(lax.rsqrt(ms + EPS), (ROW_CHUNK, LANES))
        return carry
    lax.fori_loop(0, n_rows // ROW_CHUNK, stats, 0, unroll=8)

    def scale(c, carry):
        r = pl.multiple_of(c * ROW_CHUNK, ROW_CHUNK)
        inv = inv_ref[pl.ds(r, ROW_CHUNK), :]
        for cc in range(n_cc):
            cols = slice(cc * LANES, (cc + 1) * LANES)
            x = x_ref[pl.ds(x_row0 + r, ROW_CHUNK), cols].astype(F32)
            out_ref[pl.ds(out_row0 + r, ROW_CHUNK), cols] = (
                (x * inv) * g_ref[:, cols]).astype(out_ref.dtype)
        return carry
    lax.fori_loop(0, n_rows // ROW_CHUNK, scale, 0, unroll=2)


def _cast_rows(x_ref, out_ref, n_rows):
    def body(c, carry):
        r = pl.multiple_of(c * ROW_CHUNK, ROW_CHUNK)
        out_ref[pl.ds(r, ROW_CHUNK), :] = x_ref[pl.ds(r, ROW_CHUNK), :].astype(out_ref.dtype)
        return carry
    lax.fori_loop(0, n_rows // ROW_CHUNK, body, 0, unroll=4)


def _sigmoid(x):
    return 1.0 / (1.0 + jnp.exp(-x))


def _silu(x):
    return x * _sigmoid(x)


def _gelu_tanh(x):
    c = math.sqrt(2.0 / math.pi)
    return 0.5 * x * (1.0 + jnp.tanh(c * (x + 0.044715 * (x * x * x))))


def _glu(a, gate):
    return a * _sigmoid(gate)


def _swiglu(a, b):
    return _silu(a) * b


def _identity(a):
    return a


def _norm_matmul_kernel(*refs, n_w, epilogue, tm):
    x_ref, g_ref = refs[0], refs[1]
    w_refs = refs[2:2 + n_w]
    o_ref = refs[2 + n_w]
    h_ref, inv_ref = refs[3 + n_w], refs[4 + n_w]

    @pl.when(pl.program_id(1) == 0)
    def _():
        _rms_rows(x_ref, g_ref, h_ref, inv_ref, tm)

    h = h_ref[...]
    accs = [jnp.dot(h, w[...], preferred_element_type=F32) for w in w_refs]
    o_ref[...] = epilogue(*accs).astype(o_ref.dtype)


def _norm_matmul_tiles(m, k, n_out, n_w, out_itemsize, col0s):
    for tm in (1536, 1024, 768, 512, 256, 128):
        for tn in (1024, 512, 256, 128):
            if m % tm or n_out % tn or any(c % tn for c in col0s):
                continue
            need = (2 * tm * k * 4 + tm * k * 2 + tm * LANES * 4
                    + n_w * 2 * k * tn * 2 + 2 * tm * tn * out_itemsize)
            if need <= NORM_MATMUL_VMEM_BUDGET:
                return tm, tn
    raise ValueError("no tile of the norm-matmul fits VMEM")


def _norm_matmul(x, g, ws, n_out, epilogue, out_dtype):
    m, k = x.shape
    tm, tn = _norm_matmul_tiles(m, k, n_out, len(ws), jnp.dtype(out_dtype).itemsize,
                                [c for _, _, c in ws])
    grid = (m // tm, n_out // tn)
    w_specs = [_stacked_spec(w, lead, (k, tn),
                             functools.partial(lambda i, j, cb: (0, j + cb), cb=c0 // tn))
               for w, lead, c0 in ws]
    return pl.pallas_call(
        functools.partial(_norm_matmul_kernel, n_w=len(ws), epilogue=epilogue, tm=tm),
        grid=grid,
        in_specs=[pl.BlockSpec((tm, k), lambda i, j: (i, 0)),
                  pl.BlockSpec((1, k), lambda i, j: (0, 0))] + w_specs,
        out_specs=pl.BlockSpec((tm, tn), lambda i, j: (i, j)),
        out_shape=jax.ShapeDtypeStruct((m, n_out), out_dtype),
        scratch_shapes=[pltpu.VMEM((tm, k), BF16), pltpu.VMEM((tm, LANES), F32)],
        compiler_params=_params(2),
        name="norm_matmul" + epilogue.__name__,
    )(x, g.reshape(1, k), *[w for w, _, _ in ws])


def _prologue_matmul_kernel(*refs, n_pro, n_extra, prologue):
    pro_refs = refs[:n_pro]
    w_ref, res_ref, o_ref = refs[n_pro], refs[n_pro + 1], refs[n_pro + 2]
    extra_refs = refs[n_pro + 3:n_pro + 3 + n_extra]
    scratch = refs[n_pro + 3 + n_extra:]
    a_ref = scratch[0]

    @pl.when(pl.program_id(1) == 0)
    def _():
        prologue(pro_refs, extra_refs, a_ref, scratch[1:])

    o_ref[...] = res_ref[...] + jnp.dot(a_ref[...], w_ref[...], preferred_element_type=F32)


def _residual_matmul_kernel(a_ref, w_ref, res_ref, o_ref):
    o_ref[...] = res_ref[...] + jnp.dot(a_ref[...], w_ref[...], preferred_element_type=F32)


def _residual_matmul(x_res, a, w, w_lead, tm, tn, name):
    m, k = a.shape
    n = w.shape[-1]
    return pl.pallas_call(
        _residual_matmul_kernel,
        grid=(n // tn, m // tm),
        in_specs=[pl.BlockSpec((tm, k), lambda j, i: (i, 0)),
                  _stacked_spec(w, w_lead, (k, tn), lambda j, i: (0, j)),
                  pl.BlockSpec((tm, tn), lambda j, i: (i, j))],
        out_specs=pl.BlockSpec((tm, tn), lambda j, i: (i, j)),
        out_shape=jax.ShapeDtypeStruct(x_res.shape, x_res.dtype),
        input_output_aliases={2: 0},
        compiler_params=_params(2),
        name=name,
    )(a, w, x_res)


def _prologue_matmul(x_res, w, w_lead, pro_inputs, pro_specs, prologue, *, name, tile0, n_tiles,
                     tm, tn, extra_shapes=(), extra_specs=(), scratch_shapes=()):
    k, n = w.shape[-2:]
    n_pro = len(pro_inputs)
    a_scratch = [pltpu.VMEM((tm, k), BF16)]
    return pl.pallas_call(
        functools.partial(_prologue_matmul_kernel, n_pro=n_pro, n_extra=len(extra_shapes),
                          prologue=prologue),
        grid=(n_tiles, n // tn),
        in_specs=list(pro_specs) + [
            _stacked_spec(w, w_lead, (k, tn), lambda i, j: (0, j)),
            pl.BlockSpec((tm, tn), lambda i, j: (i + tile0, j))],
        out_specs=[pl.BlockSpec((tm, tn), lambda i, j: (i + tile0, j))] + list(extra_specs),
        out_shape=[jax.ShapeDtypeStruct(x_res.shape, x_res.dtype)] + list(extra_shapes),
        scratch_shapes=a_scratch + list(scratch_shapes),
        input_output_aliases={n_pro + 1: 0},
        compiler_params=_params(2),
        name=name,
    )(*pro_inputs, w, x_res)


def _tile_spec(tm, width, tile0):
    return pl.BlockSpec((tm, width), lambda i, j: (i + tile0, 0))


def _full_spec(shape):
    return pl.BlockSpec(tuple(shape), lambda i, j: (0,) * len(shape))


def _pool_kernel(x_ref, g_ref, st_ref, w_ref, sc_ref, o_ref, tail_ref, hp_ref, sa_ref, sb_ref,
                 d_ref, inv_ref, *, tm, seg_len, carry, start_pos):
    d_model = x_ref.shape[1]
    n_groups = len(POOL_WINDOWS)
    gw = d_model // n_groups
    base = PAD + HALO
    n_seg = tm // seg_len
    i = pl.program_id(0)

    hp_ref[0:PAD, :] = jnp.zeros((PAD, d_model), F32)
    sa_ref[0:PAD, :] = jnp.zeros((PAD, gw), F32)
    sb_ref[0:PAD, :] = jnp.zeros((PAD, gw), F32)

    for s in range(n_seg):
        row0 = s * seg_len
        if carry:
            @pl.when(i == 0)
            def _():
                hp_ref[PAD:base, :] = jnp.zeros((HALO, d_model), F32)

            @pl.when(i > 0)
            def _():
                hp_ref[PAD:base, :] = hp_ref[PAD + seg_len:base + seg_len, :]
        else:
            hp_ref[PAD:base, :] = st_ref[s]

        _rms_rows(x_ref, g_ref, hp_ref, inv_ref, seg_len, x_row0=row0, out_row0=base)
        tail_ref[s] = hp_ref[base + seg_len - 16:base + seg_len, :]

        if carry:
            pos0 = start_pos + i * tm + row0
        else:
            pos0 = start_pos
        pos = (pos0 + lax.broadcasted_iota(jnp.int32, (seg_len, 1), 0)).astype(F32)

        n_steps = (HALO + seg_len) // 32
        for g, win in enumerate(POOL_WINDOWS):
            c0 = g * gw
            bufs = (sa_ref, sb_ref)
            n_stage = int(math.log2(win))
            for t in range(n_stage):
                shift = 1 << t
                dst = bufs[t % 2]
                for c in range(n_steps):
                    r = PAD + c * 32
                    if t == 0:
                        a = hp_ref[r:r + 32, c0:c0 + gw]
                        b = hp_ref[r - shift:r - shift + 32, c0:c0 + gw]
                    else:
                        src = bufs[(t - 1) % 2]
                        a = src[r:r + 32, :]
                        b = src[r - shift:r - shift + 32, :]
                    dst[r:r + 32, :] = a + b
            sums = bufs[(n_stage - 1) % 2]
            cnt = jnp.minimum(pos + 1.0, float(win))
            mean = sums[base:base + seg_len, :] / cnt
            d_ref[row0:row0 + seg_len, c0:c0 + gw] = (
                mean - hp_ref[base:base + seg_len, c0:c0 + gw]).astype(BF16)

    for g in range(n_groups):
        c0 = g * gw
        mix = jnp.dot(d_ref[:, c0:c0 + gw], w_ref[g], preferred_element_type=F32)
        o_ref[:, c0:c0 + gw] = x_ref[:, c0:c0 + gw] + mix * sc_ref[:, c0:c0 + gw]


def _pool_mix(x_all, g, state, w_bf, slot, scale, *, tile0, n_tiles, tm, seg_len, carry, start_pos):
    t_all, d = x_all.shape
    n_seg = tm // seg_len
    gw = d // len(POOL_WINDOWS)
    return pl.pallas_call(
        functools.partial(_pool_kernel, tm=tm, seg_len=seg_len, carry=carry, start_pos=start_pos),
        grid=(n_tiles,),
        in_specs=[pl.BlockSpec((tm, d), lambda i: (i + tile0, 0)),
                  pl.BlockSpec((1, d), lambda i: (0, 0)),
                  pl.BlockSpec(state.shape, lambda i: (0, 0, 0)),
                  _stacked_spec(w_bf, (slot,), w_bf.shape[1:], lambda i: (0, 0, 0)),
                  pl.BlockSpec((1, d), lambda i: (0, 0))],
        out_specs=[pl.BlockSpec((tm, d), lambda i: (i + tile0, 0)),
                   pl.BlockSpec((n_seg, 16, d), lambda i: (0, 0, 0))],
        out_shape=[jax.ShapeDtypeStruct((t_all, d), F32),
                   jax.ShapeDtypeStruct((n_seg, 16, d), F32)],
        scratch_shapes=[pltpu.VMEM((PAD + HALO + seg_len, d), F32),
                        pltpu.VMEM((PAD + HALO + seg_len, gw), F32),
                        pltpu.VMEM((PAD + HALO + seg_len, gw), F32),
                        pltpu.VMEM((tm, d), BF16),
                        pltpu.VMEM((seg_len, LANES), F32)],
        input_output_aliases={0: 0},
        compiler_params=_params(1),
        name="pool_mix_carry" if carry else "pool_mix_state",
    )(x_all, g.reshape(1, d), state, w_bf, scale.reshape(1, d))


CONV_STRIDE = 4


def _conv_prologue(pro_refs, extra_refs, a_ref, scratch, *, tm, seg_len, carry, width):
    u_ref, st_ref, wb_ref, bb_ref, gn_ref = pro_refs
    up_ref, c_ref = scratch
    d_model = u_ref.shape[1]
    n_slabs = d_model // LANES
    n_seg = tm // seg_len
    first = HALO - (width - 1)
    rows_per_step = SUBLANES * CONV_STRIDE
    i = pl.program_id(0)

    for s in range(n_seg):
        row0 = s * seg_len
        if carry:
            @pl.when(i == 0)
            def _():
                up_ref[:, 0:HALO, :] = jnp.zeros((n_slabs, HALO, LANES), F32)

            @pl.when(i > 0)
            def _():
                up_ref[:, 0:HALO, :] = up_ref[:, seg_len:seg_len + HALO, :]
        else:
            up_ref[:, 0:HALO, :] = st_ref[s]
        for sl in range(n_slabs):
            up_ref[sl, HALO:HALO + seg_len, :] = u_ref[row0:row0 + seg_len, sl * LANES:(sl + 1) * LANES]

        def slab_body(sl, cr):
            def step_body(c, cr2):
                r0 = pl.multiple_of(c * rows_per_step, rows_per_step)
                accs = [[bb_ref[sl], jnp.zeros((SUBLANES, LANES), F32)] for _ in range(CONV_STRIDE)]
                for m in range(width + CONV_STRIDE - 1):
                    x = up_ref[sl, pl.ds(r0 + first + m, SUBLANES, stride=CONV_STRIDE), :]
                    for ph in range(CONV_STRIDE):
                        k = m - ph
                        if 0 <= k < width:
                            accs[ph][k % 2] = accs[ph][k % 2] + (
                                x * wb_ref[sl, SUBLANES * k:SUBLANES * (k + 1), :])
                for ph in range(CONV_STRIDE):
                    c_ref[sl, pl.ds(r0 + ph, SUBLANES, stride=CONV_STRIDE), :] = accs[ph][0] + accs[ph][1]
                return cr2
            lax.fori_loop(0, seg_len // rows_per_step, step_body, 0, unroll=2)
            return cr
        lax.fori_loop(0, n_slabs, slab_body, 0)

        def norm_body(c, cr):
            r = pl.multiple_of(c * ROW_CHUNK, ROW_CHUNK)
            part = jnp.zeros((ROW_CHUNK, LANES), F32)
            for sl in range(n_slabs):
                v = c_ref[sl, pl.ds(r, ROW_CHUNK), :]
                part = part + v * v
            inv = lax.rsqrt(jnp.sum(part, axis=-1, keepdims=True) / d_model + EPS)
            for sl in range(n_slabs):
                cols = slice(sl * LANES, (sl + 1) * LANES)
                y = (c_ref[sl, pl.ds(r, ROW_CHUNK), :] * inv) * gn_ref[:, cols]
                a_ref[pl.ds(row0 + r, ROW_CHUNK), cols] = _silu(y).astype(BF16)
            return cr
        lax.fori_loop(0, seg_len // ROW_CHUNK, norm_body, 0, unroll=2)


def _to_slabs(a):
    *lead, rows, d = a.shape
    a = a.reshape(*lead, rows, d // LANES, LANES)
    return jnp.swapaxes(a, -3, -2)


GM_COLS = 512


def _gmlp_prologue(pro_refs, extra_refs, a_ref, scratch, *, tm, emit_v):
    z_ref, gv_ref, ws_ref, bs_ref = pro_refs
    (vn_ref,) = scratch
    d_gm = gv_ref.shape[1]
    gcols = d_gm // GM_GROUPS
    n_cc = d_gm // 1024

    def norm_body(c, cr):
        r = pl.multiple_of(c * ROW_CHUNK, ROW_CHUNK)
        ssq = jnp.zeros((ROW_CHUNK, 1), F32)
        for cc in range(n_cc):
            v = z_ref[pl.ds(r, ROW_CHUNK), d_gm + cc * 1024:d_gm + (cc + 1) * 1024].astype(F32)
            ssq = ssq + jnp.sum(v * v, axis=-1, keepdims=True)
        inv = lax.rsqrt(ssq / d_gm + EPS)
        for cc in range(n_cc):
            cols = slice(cc * 1024, (cc + 1) * 1024)
            v = z_ref[pl.ds(r, ROW_CHUNK), d_gm + cc * 1024:d_gm + (cc + 1) * 1024].astype(F32)
            vn = (v * inv) * gv_ref[:, cols]
            vn_ref[pl.ds(r, ROW_CHUNK), cols] = vn.astype(BF16)
            if emit_v:
                extra_refs[0][pl.ds(r, ROW_CHUNK), cols] = vn
        return cr
    lax.fori_loop(0, tm // ROW_CHUNK, norm_body, 0, unroll=2)

    for g in range(GM_GROUPS):
        for cc in range(gcols // GM_COLS):
            cols = slice(g * gcols + cc * GM_COLS, g * gcols + (cc + 1) * GM_COLS)
            s = jnp.dot(ws_ref[g], vn_ref[:, cols], preferred_element_type=F32)
            s = s + bs_ref[:, g:g + 1]
            a_ref[:, cols] = (z_ref[:, cols].astype(F32) * s).astype(BF16)


def _gmlp_spatial_weights(w_s, b_s, blk, tm):
    cidx = jnp.arange(blk) // CHUNK
    mask = cidx[None, :] <= cidx[:, None]
    ws = jnp.where(mask[None], w_s[:, :blk, :blk], 0.0)
    eye = jnp.eye(tm // blk, dtype=ws.dtype)
    big = jnp.einsum("ab,gij->gaibj", eye, ws).reshape(w_s.shape[0], tm, tm)
    bias = jnp.tile(b_s[:, :blk].T, (tm // blk, 1))
    return big.astype(BF16), bias.astype(F32)


def _attn_prologue(pro_refs, extra_refs, a_ref, scratch, *, segments, n_heads):
    q_ref, k_ref, v_ref = pro_refs
    d_model = q_ref.shape[1]
    hd = d_model // n_heads
    scale = hd ** -0.5
    for row0, n_rows, kv in segments:
        for h in range(n_heads):
            cols = slice(h * hd, (h + 1) * hd)
            q = q_ref[row0:row0 + n_rows, cols]
            s = lax.dot_general(q, k_ref[kv, :, cols], (((1,), (1,)), ((), ())),
                                preferred_element_type=F32) * scale
            m = jnp.max(s, axis=-1, keepdims=True)
            e = jnp.exp(s - m)
            p = e / jnp.sum(e, axis=-1, keepdims=True)
            o = jnp.dot(p.astype(BF16), v_ref[kv, :, cols], preferred_element_type=F32)
            a_ref[row0:row0 + n_rows, cols] = o.astype(BF16)


def _router_kernel(x_ref, g_ref, wr_ref, h_ref, meta_ref, gate_ref, cnt_ref, run_ref, inv_ref, *,
                   tm, n_experts):
    i = pl.program_id(0)

    @pl.when(i == 0)
    def _():
        run_ref[...] = jnp.zeros(run_ref.shape, F32)

    _rms_rows(x_ref, g_ref, h_ref, inv_ref, tm)
    logits = jnp.dot(h_ref[...], wr_ref[...], preferred_element_type=F32,
                     precision=lax.Precision.HIGHEST)
    lane = lax.broadcasted_iota(jnp.int32, (tm, LANES), 1)
    neg = jnp.float32(-jnp.inf)
    logits = jnp.where(lane < n_experts, logits, neg)
    m1 = jnp.max(logits, axis=-1, keepdims=True)
    i1 = jnp.min(jnp.where(logits == m1, lane, LANES), axis=-1, keepdims=True)
    rest = jnp.where(lane == i1, neg, logits)
    m2 = jnp.max(rest, axis=-1, keepdims=True)
    i2 = jnp.min(jnp.where(rest == m2, lane, LANES), axis=-1, keepdims=True)
    e2 = jnp.exp(m2 - m1)
    den = 1.0 + e2
    g1 = 1.0 / den
    g2 = e2 / den

    sel1 = lane == i1
    sel2 = lane == i2
    onehot = jnp.where(sel1 | sel2, 1.0, 0.0)
    r_i = lax.broadcasted_iota(jnp.int32, (tm, tm), 0)
    c_i = lax.broadcasted_iota(jnp.int32, (tm, tm), 1)
    tri = jnp.where(c_i < r_i, 1.0, 0.0).astype(BF16)
    before = jnp.dot(tri, onehot.astype(BF16), preferred_element_type=F32) + run_ref[0:1, :]
    rank1 = jnp.sum(jnp.where(sel1, before, 0.0), axis=-1, keepdims=True)
    rank2 = jnp.sum(jnp.where(sel2, before, 0.0), axis=-1, keepdims=True)
    run_ref[0:1, :] = run_ref[0:1, :] + jnp.sum(onehot, axis=0, keepdims=True)
    cnt_ref[...] = jnp.broadcast_to(run_ref[0:1, :], cnt_ref.shape)

    col = lax.broadcasted_iota(jnp.int32, (tm, 8), 1)
    meta_ref[...] = jnp.where(col == 0, i1.astype(F32),
                    jnp.where(col == 1, i2.astype(F32),
                    jnp.where(col == 2, rank1,
                    jnp.where(col == 3, rank2, 0.0))))
    gate_ref[:, 0:LANES] = jnp.broadcast_to(g1, (tm, LANES))
    gate_ref[:, LANES:2 * LANES] = jnp.broadcast_to(g2, (tm, LANES))


def _router(x_all, g, w_r, tm):
    t_all, d = x_all.shape
    n_experts = w_r.shape[1]
    wr_pad = jnp.zeros((d, LANES), F32).at[:, :n_experts].set(w_r)
    return pl.pallas_call(
        functools.partial(_router_kernel, tm=tm, n_experts=n_experts),
        grid=(t_all // tm,),
        in_specs=[pl.BlockSpec((tm, d), lambda i: (i, 0)),
                  pl.BlockSpec((1, d), lambda i: (0, 0)),
                  pl.BlockSpec((d, LANES), lambda i: (0, 0))],
        out_specs=[pl.BlockSpec((tm, d), lambda i: (i, 0)),
                   pl.BlockSpec((tm, 8), lambda i: (i, 0)),
                   pl.BlockSpec((tm, 2 * LANES), lambda i: (i, 0)),
                   pl.BlockSpec((8, LANES), lambda i: (0, 0))],
        out_shape=[jax.ShapeDtypeStruct((t_all, d), F32),
                   jax.ShapeDtypeStruct((t_all, 8), F32),
                   jax.ShapeDtypeStruct((t_all, 2 * LANES), F32),
                   jax.ShapeDtypeStruct((8, LANES), F32)],
        scratch_shapes=[pltpu.VMEM((8, LANES), F32), pltpu.VMEM((tm, LANES), F32)],
        compiler_params=_params(1),
        name="moe_router",
    )(x_all, g.reshape(1, d), wr_pad)


def _dispatch_kernel(p1_ref, p2_ref, ps_ref, pc_ref, nu_ref, h_ref, buf_ref, zero_ref, sem, *, tm,
                     n_experts, n_tiles):
    t0 = pl.program_id(0) * tm

    @pl.when(pl.program_id(0) == 0)
    def _():
        zero_ref[...] = jnp.zeros(zero_ref.shape, zero_ref.dtype)
        for e in range(n_experts):
            def pad_copy(r, e=e):
                return pltpu.make_async_copy(zero_ref.at[pl.ds(0, 1)],
                                             buf_ref.at[pl.ds(ps_ref[e] + r, 1)], sem)

            def pad_start(r, cr):
                pad_copy(r).start()
                return cr
            lax.fori_loop(0, pc_ref[e], pad_start, 0)

            def pad_wait(r, cr):
                pad_copy(r).wait()
                return cr
            lax.fori_loop(0, pc_ref[e], pad_wait, 0)

        def tile_copy(k):
            return pltpu.make_async_copy(zero_ref, buf_ref.at[pl.ds(k * tm, tm)], sem)

        def tile_start(k, cr):
            tile_copy(k).start()
            return cr
        lax.fori_loop(nu_ref[0], n_tiles, tile_start, 0)

        def tile_wait(k, cr):
            tile_copy(k).wait()
            return cr
        lax.fori_loop(nu_ref[0], n_tiles, tile_wait, 0)

    def copies(r):
        t = t0 + r
        return (pltpu.make_async_copy(h_ref.at[pl.ds(r, 1)], buf_ref.at[pl.ds(p1_ref[t], 1)], sem),
                pltpu.make_async_copy(h_ref.at[pl.ds(r, 1)], buf_ref.at[pl.ds(p2_ref[t], 1)], sem))

    def start(r, cr):
        for c in copies(r):
            c.start()
        return cr
    lax.fori_loop(0, tm, start, 0, unroll=8)

    def wait(r, cr):
        for c in copies(r):
            c.wait()
        return cr
    lax.fori_loop(0, tm, wait, 0, unroll=8)


def _dispatch(h, p1, p2, pad_start, pad_count, n_used, n_slots, tm):
    t_all, d = h.shape
    return pl.pallas_call(
        functools.partial(_dispatch_kernel, tm=tm, n_experts=pad_start.shape[0],
                          n_tiles=n_slots // tm),
        grid_spec=pltpu.PrefetchScalarGridSpec(
            num_scalar_prefetch=5,
            grid=(t_all // tm,),
            in_specs=[pl.BlockSpec((tm, d), lambda i, *_: (i, 0))],
            out_specs=pl.BlockSpec(memory_space=pl.ANY),
            scratch_shapes=[pltpu.VMEM((tm, d), h.dtype), pltpu.SemaphoreType.DMA(())]),
        out_shape=jax.ShapeDtypeStruct((n_slots, d), h.dtype),
        compiler_params=_params(1),
        name="moe_dispatch",
    )(p1, p2, pad_start, pad_count, n_used, h)


def _expert_up_kernel(te_ref, nu_ref, x_ref, w1_ref, w3_ref, o_ref, h_ref, *, tm, col_splits):
    i = pl.program_id(0)

    @pl.when(i < nu_ref[0])
    def _():
        _cast_rows(x_ref, h_ref, tm)
        h = h_ref[...]
        for c0, c1 in col_splits:
            a = jnp.dot(h, w1_ref[:, c0:c1], preferred_element_type=F32)
            b = jnp.dot(h, w3_ref[:, c0:c1], preferred_element_type=F32)
            o_ref[:, c0:c1] = _swiglu(a, b).astype(o_ref.dtype)

    @pl.when(i >= nu_ref[0])
    def _():
        o_ref[...] = jnp.zeros(o_ref.shape, o_ref.dtype)


def _expert_up(buf, w1, w3, layer, tile_expert, n_used, tm):
    n_slots, d = buf.shape
    f = w1.shape[-1]
    half = (f // MXU_COLS + 1) // 2 * MXU_COLS
    col_splits = ((0, half), (half, f)) if 0 < half < f else ((0, f),)
    w_spec = lambda: pl.BlockSpec((None, None, d, f), lambda i, te, nu: (layer, te[i], 0, 0),
                                  pipeline_mode=pl.Buffered(1))
    return pl.pallas_call(
        functools.partial(_expert_up_kernel, tm=tm, col_splits=col_splits),
        grid_spec=pltpu.PrefetchScalarGridSpec(
            num_scalar_prefetch=2,
            grid=(n_slots // tm,),
            in_specs=[pl.BlockSpec((tm, d), lambda i, te, nu: (i, 0)), w_spec(), w_spec()],
            out_specs=pl.BlockSpec((tm, f), lambda i, te, nu: (i, 0)),
            scratch_shapes=[pltpu.VMEM((tm, d), BF16)]),
        out_shape=jax.ShapeDtypeStruct((n_slots, f), BF16),
        compiler_params=_params(1),
        name="moe_expert_up",
    )(tile_expert, n_used, buf, w1, w3)


def _expert_down_kernel(te_ref, nu_ref, x_ref, w_ref, o_ref, wb_ref, *, f):
    i = pl.program_id(1)
    used = i < nu_ref[0]
    new_block = jnp.logical_or(i == 0, te_ref[i] != te_ref[jnp.maximum(i - 1, 0)])

    @pl.when(jnp.logical_and(used, new_block))
    def _():
        _cast_rows(w_ref, wb_ref, f)

    @pl.when(used)
    def _():
        o_ref[...] = jnp.dot(x_ref[...], wb_ref[...], preferred_element_type=F32)

    @pl.when(jnp.logical_not(used))
    def _():
        o_ref[...] = jnp.zeros(o_ref.shape, o_ref.dtype)


def _expert_down(t_sorted, w2, layer, tile_expert, n_used, tm, tn):
    n_slots, f = t_sorted.shape
    d = w2.shape[-1]
    return pl.pallas_call(
        functools.partial(_expert_down_kernel, f=f),
        grid_spec=pltpu.PrefetchScalarGridSpec(
            num_scalar_prefetch=2,
            grid=(d // tn, n_slots // tm),
            in_specs=[pl.BlockSpec((tm, f), lambda j, i, te, nu: (i, 0)),
                      pl.BlockSpec((None, None, f, tn), lambda j, i, te, nu: (layer, te[i], 0, j))],
            out_specs=pl.BlockSpec((tm, tn), lambda j, i, te, nu: (i, j)),
            scratch_shapes=[pltpu.VMEM((f, tn), BF16)]),
        out_shape=jax.ShapeDtypeStruct((n_slots, d), F32),
        compiler_params=_params(2),
        name="moe_expert_down",
    )(tile_expert, n_used, t_sorted, w2)


def _combine_kernel(p1_ref, p2_ref, x_ref, gate_ref, y_ref, gout_ref, *rest, tm, n_tiles, final_norm):
    o_refs, (a_ref, b_ref, sems) = rest[:-3], rest[-3:]
    i = pl.program_id(0)
    slot = i % 2

    def copies(tile, sl, r):
        t = tile * tm + r
        return (pltpu.make_async_copy(y_ref.at[pl.ds(p1_ref[t], 1)], a_ref.at[sl, pl.ds(r, 1)],
                                      sems.at[sl]),
                pltpu.make_async_copy(y_ref.at[pl.ds(p2_ref[t], 1)], b_ref.at[sl, pl.ds(r, 1)],
                                      sems.at[sl]))

    def start_gather(tile, sl):
        def start(r, cr):
            for c in copies(tile, sl, r):
                c.start()
            return cr
        lax.fori_loop(0, tm, start, 0, unroll=8)

    @pl.when(i == 0)
    def _():
        start_gather(0, 0)

    @pl.when(i + 1 < n_tiles)
    def _():
        start_gather(i + 1, 1 - slot)

    def wait(r, cr):
        for c in copies(i, slot, r):
            c.wait()
        return cr
    lax.fori_loop(0, tm, wait, 0, unroll=8)

    d = x_ref.shape[1]
    n_cc = d // LANES

    def body(c, cr, o_ref):
        r = pl.multiple_of(c * ROW_CHUNK, ROW_CHUNK)
        rows = pl.ds(r, ROW_CHUNK)
        g1 = gate_ref[rows, 0:LANES]
        g2 = gate_ref[rows, LANES:2 * LANES]
        part = jnp.zeros((ROW_CHUNK, LANES), F32)
        for cc in range(n_cc):
            cols = slice(cc * LANES, (cc + 1) * LANES)
            x = x_ref[rows, cols] + (g1 * a_ref[slot, rows, cols] + g2 * b_ref[slot, rows, cols])
            o_ref[rows, cols] = x
            if final_norm:
                part = part + x * x
        if final_norm:
            ms = jnp.sum(part, axis=-1, keepdims=True) / d
            inv = jnp.broadcast_to(lax.rsqrt(ms + EPS), (ROW_CHUNK, LANES))
            for cc in range(n_cc):
                cols = slice(cc * LANES, (cc + 1) * LANES)
                o_ref[rows, cols] = (o_ref[rows, cols] * inv) * gout_ref[:, cols]
        return cr

    def run(o_ref):
        lax.fori_loop(0, tm // ROW_CHUNK, functools.partial(body, o_ref=o_ref), 0, unroll=2)

    if final_norm:
        @pl.when(i < n_tiles - 1)
        def _():
            run(o_refs[0])

        @pl.when(i == n_tiles - 1)
        def _():
            run(o_refs[1])
    else:
        run(o_refs[0])


def _combine(x_all, gates, y_sorted, p1, p2, g_out, tm, final_norm):
    t_all, d = x_all.shape
    n_tiles = t_all // tm
    if final_norm:
        out_specs = [pl.BlockSpec((tm, d), lambda i, a, b: (jnp.minimum(i, n_tiles - 2), 0)),
                     pl.BlockSpec((tm, d), lambda i, a, b: (0, 0))]
        out_shape = [jax.ShapeDtypeStruct((t_all - tm, d), F32), jax.ShapeDtypeStruct((tm, d), F32)]
    else:
        out_specs = pl.BlockSpec((tm, d), lambda i, a, b: (i, 0))
        out_shape = jax.ShapeDtypeStruct((t_all, d), F32)
    return pl.pallas_call(
        functools.partial(_combine_kernel, tm=tm, n_tiles=t_all // tm, final_norm=final_norm),
        grid_spec=pltpu.PrefetchScalarGridSpec(
            num_scalar_prefetch=2,
            grid=(t_all // tm,),
            in_specs=[pl.BlockSpec((tm, d), lambda i, a, b: (i, 0)),
                      pl.BlockSpec((tm, 2 * LANES), lambda i, a, b: (i, 0)),
                      pl.BlockSpec(memory_space=pl.ANY),
                      pl.BlockSpec((1, d), lambda i, a, b: (0, 0))],
            out_specs=out_specs,
            scratch_shapes=[pltpu.VMEM((2, tm, d), F32), pltpu.VMEM((2, tm, d), F32),
                            pltpu.SemaphoreType.DMA((2,))]),
        out_shape=out_shape,
        input_output_aliases={} if final_norm else {2: 0},
        compiler_params=_params(1),
        name="moe_combine_norm" if final_norm else "moe_combine",
    )(p1, p2, x_all, gates, y_sorted, g_out.reshape(1, d))


def _moe_ffn(x_all, g, w_r, w1, w3, w2, layer, g_out, tm, final_norm):
    t_all, d = x_all.shape
    n_experts = w_r.shape[1]
    h, meta, gates, counts = _router(x_all, g, w_r, tm)

    cnt = counts[0, :n_experts].astype(jnp.int32)
    padded = ((cnt + tm - 1) // tm) * tm
    ends = jnp.cumsum(padded)
    offs = ends - padded
    e1 = meta[:, 0].astype(jnp.int32)
    e2 = meta[:, 1].astype(jnp.int32)
    p1 = offs[e1] + meta[:, 2].astype(jnp.int32)
    p2 = offs[e2] + meta[:, 3].astype(jnp.int32)
    n_tiles = (TOP_K * t_all) // tm + n_experts
    tile_start = jnp.arange(n_tiles, dtype=jnp.int32) * tm
    tile_expert = jnp.minimum(
        jnp.sum((tile_start[:, None] >= ends[None, :]).astype(jnp.int32), axis=1), n_experts - 1)
    n_used = (ends[-1] // tm).reshape(1).astype(jnp.int32)

    buf = _dispatch(h, p1, p2, offs + cnt, padded - cnt, n_used, n_tiles * tm, tm)
    f = w1.shape[-1]
    t_sorted = _expert_up(buf, w1, w3, layer, tile_expert, n_used, tm)
    y_sorted = _expert_down(t_sorted, w2, layer, tile_expert, n_used, tm,
                            _pick(d, (1024, 512, 256, 128)))
    return _combine(x_all, gates, y_sorted, p1, p2, g_out, tm, final_norm)


def kernel(x_prompt, x_sample, cache_mem_k, cache_mem_v, state_pool, state_conv, mem_prompt,
           norm_mix, norm_xattn, norm_mem, norm_ffn, norm_out, pool_w, pool_scale,
           conv_w_in, conv_w_dw, conv_b_dw, conv_norm, conv_w_out,
           gm_w_in, gm_norm, gm_w_s, gm_b_s, gm_w_out, xa_w_q, xa_w_kv, xa_w_o,
           ffn_w1, ffn_w3, ffn_w2, moe_router, moe_w1, moe_w3, moe_w2):
    b_p, seq, d = x_prompt.shape
    b_s, dec_seq, _ = x_sample.shape
    depth = norm_mix.shape[0]
    mem_len = mem_prompt.shape[1]
    n_heads = cache_mem_k.shape[3]
    pool_state = state_pool.shape[2]
    conv_width = conv_w_dw.shape[1]
    conv_state = state_conv.shape[2]
    d_gm = gm_norm.shape[1]
    d_ff = ffn_w1.shape[2]
    tm = TILE_M
    assert b_p == 1 and seq % tm == 0 and b_s * dec_seq == tm
    assert conv_state <= HALO and pool_state <= 16 and seq >= conv_state
    assert dec_seq % (SUBLANES * CONV_STRIDE) == 0 and d % LANES == 0
    assert depth % 2 == 0, "the output norm is fused into the last layer's expert FFN"
    n_ptiles = seq // tm
    t_p = seq
    t_all = seq + tm

    x = jnp.concatenate([x_prompt.reshape(seq, d), x_sample.reshape(tm, d)], axis=0)
    bf = lambda a: a.astype(BF16)
    pool_w, conv_w_in, conv_w_out, gm_w_in, gm_w_out = map(
        bf, (pool_w, conv_w_in, conv_w_out, gm_w_in, gm_w_out))
    xa_w_q, xa_w_kv, xa_w_o, ffn_w1, ffn_w3, ffn_w2, moe_w1, moe_w3 = map(
        bf, (xa_w_q, xa_w_kv, xa_w_o, ffn_w1, ffn_w3, ffn_w2, moe_w1, moe_w3))

    tn_d = _pick(d, (1024, 512, 256, 128))
    k_cache = bf(cache_mem_k.reshape(depth, b_s, mem_len, d))
    v_cache = bf(cache_mem_v.reshape(depth, b_s, mem_len, d))

    mem = mem_prompt.reshape(mem_len, d)
    kv_p = [_norm_matmul(mem, norm_mem[i], [(xa_w_kv, (i,), 0)], 2 * d, _identity, F32)
            for i in range(depth)]
    new_mem_k = jnp.stack([kv[:, :d] for kv in kv_p]).reshape(depth, 1, mem_len, n_heads, d // n_heads)
    new_mem_v = jnp.stack([kv[:, d:] for kv in kv_p]).reshape(depth, 1, mem_len, n_heads, d // n_heads)

    pool_tails_p, pool_tails_s, conv_p, conv_s, gm_s = [], [], [], [], []
    for i in range(depth):
        slot = i // 3
        mixer = i % 3
        if mixer == 0:
            zero_state = jnp.zeros((1, HALO, d), F32)
            x, tail_p = _pool_mix(x, norm_mix[i], zero_state, pool_w, slot, pool_scale[slot], tile0=0,
                                  n_tiles=n_ptiles, tm=tm, seg_len=tm, carry=True, start_pos=0)
            st = jnp.zeros((b_s, HALO, d), F32).at[:, HALO - pool_state:].set(state_pool[slot])
            x, tail_s = _pool_mix(x, norm_mix[i], st, pool_w, slot, pool_scale[slot], tile0=n_ptiles,
                                  n_tiles=1, tm=tm, seg_len=dec_seq, carry=False,
                                  start_pos=PAST_LEN)
            pool_tails_p.append(tail_p[:, 16 - pool_state:])
            pool_tails_s.append(tail_s[:, 16 - pool_state:])
        elif mixer == 1:
            u = _norm_matmul(x, norm_mix[i], [(conv_w_in, (slot,), 0), (conv_w_in, (slot,), d)],
                             d, _glu, F32)
            n_slabs = d // LANES
            wb = _to_slabs(jnp.repeat(conv_w_dw[slot], SUBLANES, axis=0))
            bb = _to_slabs(jnp.broadcast_to(conv_b_dw[slot][None], (SUBLANES, d)))
            small = [wb, bb, conv_norm[slot].reshape(1, d)]
            small_specs = [_full_spec(wb.shape), _full_spec(bb.shape), _full_spec((1, d))]
            st_s = jnp.zeros((b_s, HALO, d), F32).at[:, HALO - conv_state:].set(state_conv[slot])
            for tile0, n_tiles, seg_len, carry, st in (
                    (0, n_ptiles, tm, True, jnp.zeros((1, n_slabs, HALO, LANES), F32)),
                    (n_ptiles, 1, dec_seq, False, _to_slabs(st_s))):
                (x,) = _prologue_matmul(
                    x, conv_w_out, (slot,), [u, st] + small,
                    [_tile_spec(tm, d, tile0), _full_spec(st.shape)] + small_specs,
                    functools.partial(_conv_prologue, tm=tm, seg_len=seg_len, carry=carry,
                                      width=conv_width),
                    name="conv_out_carry" if carry else "conv_out_state",
                    tile0=tile0, n_tiles=n_tiles, tm=tm, tn=tn_d,
                    scratch_shapes=[pltpu.VMEM((n_slabs, HALO + seg_len, LANES), F32),
                                    pltpu.VMEM((n_slabs, seg_len, LANES), F32)])
            conv_p.append(u[t_p - conv_state:t_p].reshape(1, conv_state, d))
            conv_s.append(u[t_p:].reshape(b_s, dec_seq, d)[:, dec_seq - conv_state:])
        else:
            z = _norm_matmul(x, norm_mix[i], [(gm_w_in, (slot,), 0)], 2 * d_gm, _gelu_tanh, BF16)
            tn_g = _pick(d, (512, 256, 128))
            for tile0, n_tiles, blk, emit_v in ((0, n_ptiles, min(seq, GM_BLOCK), False),
                                                (n_ptiles, 1, min(dec_seq, GM_BLOCK), True)):
                ws_big, bias = _gmlp_spatial_weights(gm_w_s[slot], gm_b_s[slot], blk, tm)
                outs = _prologue_matmul(
                    x, gm_w_out, (slot,), [z, gm_norm[slot].reshape(1, d_gm), ws_big, bias],
                    [_tile_spec(tm, 2 * d_gm, tile0), _full_spec((1, d_gm)),
                     _full_spec(ws_big.shape), _full_spec(bias.shape)],
                    functools.partial(_gmlp_prologue, tm=tm, emit_v=emit_v),
                    name="gmlp_out_state" if emit_v else "gmlp_out",
                    tile0=tile0, n_tiles=n_tiles, tm=tm, tn=tn_g,
                    extra_shapes=[jax.ShapeDtypeStruct((tm, d_gm), F32)] if emit_v else [],
                    extra_specs=[_full_spec((tm, d_gm))] if emit_v else [],
                    scratch_shapes=[pltpu.VMEM((tm, d_gm), BF16)])
                x = outs[0]
                if emit_v:
                    gm_s.append(outs[1].reshape(b_s, dec_seq, d_gm))

        q = _norm_matmul(x, norm_xattn[i], [(xa_w_q, (i,), 0)], d, _identity, BF16)
        kv = kv_p[i]
        k_p, v_p = bf(kv[:, :d]).reshape(1, 1, mem_len, d), bf(kv[:, d:]).reshape(1, 1, mem_len, d)
        half = tm // 2
        for tile0, n_tiles, kk, vv, lead, segments in (
                (0, n_ptiles, k_p, v_p, 0, ((0, half, 0), (half, half, 0))),
                (n_ptiles, 1, k_cache, v_cache, i,
                 tuple((b * dec_seq, dec_seq, b) for b in range(b_s)))):
            kv_spec = lambda a: _stacked_spec(a, (lead,), a.shape[1:], lambda i_, j_: (0, 0, 0))
            (x,) = _prologue_matmul(
                x, xa_w_o, (i,), [q, kk, vv],
                [_tile_spec(tm, d, tile0), kv_spec(kk), kv_spec(vv)],
                functools.partial(_attn_prologue, segments=segments, n_heads=n_heads),
                name="attn_out_shared_kv" if tile0 == 0 else "attn_out_batched_kv",
                tile0=tile0, n_tiles=n_tiles, tm=tm, tn=d if tile0 == 0 else tn_d)

        j = i // 2
        if i % 2 == 0:
            t = _norm_matmul(x, norm_ffn[i], [(ffn_w1, (j,), 0), (ffn_w3, (j,), 0)], d_ff, _swiglu,
                             BF16)
            x = _residual_matmul(x, t, ffn_w2, (j,), tm, tn_d, "ffn_down")
        else:
            x = _moe_ffn(x, norm_ffn[i], moe_router[j], moe_w1, moe_w3, moe_w2, j, norm_out, tm,
                         final_norm=(i == depth - 1))

    y_prompt = x[0].reshape(1, seq, d)
    y_sample = x[1].reshape(b_s, dec_seq, d)
    return (y_prompt, y_sample, new_mem_k, new_mem_v,
            jnp.stack(pool_tails_p), jnp.stack(pool_tails_s),
            jnp.stack(conv_p), jnp.stack(conv_s), jnp.stack(gm_s))
```
